```python
import math
import jax, jax.numpy as jnp
from jax import lax
import numpy as np

D_MODEL = 1024
BATCH = 16
SEQ = 4096
DEPTH = 2

N_MIXERS = 2
DA_HEADS = 8
DA_HEAD_DIM = 64
DA_VDIM = 2 * DA_HEAD_DIM
Q_BLOCK = 128
REL_BUCKETS = 32
REL_MAX_DIST = 128
SG_CHUNK = 128
SG_GROUPS = 8
SG_HALF = 3 * D_MODEL
SG_GROUP_DIM = SG_HALF // SG_GROUPS
MOE_GROUPS = 8
MOE_PER_GROUP = 8
MOE_EXPERTS = MOE_GROUPS * MOE_PER_GROUP
MOE_TOPK = 2
MOE_HIDDEN = D_MODEL // 2
MOE_BLOCK = 128
LN_EPS = 1e-5
DN_ALPHA = (2 * DEPTH) ** 0.25
DN_BETA = (8 * DEPTH) ** -0.25
N_ATTN_LAYERS = (DEPTH + 1) // 2
N_SG_LAYERS = DEPTH // 2

kernel_name = "hybrid_diffattn_gmlp_hmoe_deepnorm"


def layer_norm(x, g, b):
    xf = x.astype(jnp.float32)
    mu = jnp.mean(xf, axis=-1, keepdims=True)
    var = jnp.mean(jnp.square(xf - mu), axis=-1, keepdims=True)
    y = (xf - mu) * lax.rsqrt(var + LN_EPS)
    return (y * g.astype(jnp.float32) + b.astype(jnp.float32)).astype(x.dtype)


def rms_norm(x, g):
    xf = x.astype(jnp.float32)
    y = xf * lax.rsqrt(jnp.mean(jnp.square(xf), axis=-1, keepdims=True) + LN_EPS)
    return (y * g.astype(jnp.float32)).astype(x.dtype)


def t5_bucket(dist):
    n = jnp.maximum(dist, 0)
    max_exact = REL_BUCKETS // 2
    nf = jnp.maximum(n, 1).astype(jnp.float32)
    large = max_exact + (jnp.log(nf / max_exact) / math.log(REL_MAX_DIST / max_exact)
                         * (REL_BUCKETS - max_exact)).astype(jnp.int32)
    large = jnp.minimum(large, REL_BUCKETS - 1)
    return jnp.where(n < max_exact, n, large)


def diff_attention(x, w_in, w_out, lam_vecs, subln_g, rel_table, lam_init):
    B, S, _ = x.shape
    H, d = DA_HEADS, DA_HEAD_DIM
    qk_w = H * 2 * d
    qkv = x @ w_in
    q = qkv[..., :qk_w].reshape(B, S, H, 2, d).transpose(0, 2, 3, 1, 4) * (d ** -0.5)
    k = qkv[..., qk_w:2 * qk_w].reshape(B, S, H, 2, d).transpose(0, 2, 3, 1, 4)
    v = qkv[..., 2 * qk_w:].reshape(B, S, H, DA_VDIM).transpose(0, 2, 1, 3)
    lv = lam_vecs.astype(jnp.float32)
    lam = jnp.exp(jnp.sum(lv[0] * lv[1])) - jnp.exp(jnp.sum(lv[2] * lv[3])) + lam_init
    nq = S // Q_BLOCK
    q_blocks = jnp.moveaxis(q.reshape(B, H, 2, nq, Q_BLOCK, d), 3, 0)
    kpos = jnp.arange(S, dtype=jnp.int32)

    def block_fn(args):
        qb, i = args
        qpos = i * Q_BLOCK + jnp.arange(Q_BLOCK, dtype=jnp.int32)
        dist = qpos[:, None] - kpos[None, :]
        bias = jnp.transpose(rel_table[t5_bucket(dist)], (2, 0, 1))
        logits = jnp.einsum('bhmqd,bhmkd->bhmqk', qb, k).astype(jnp.float32)
        logits = logits + bias[None, :, None].astype(jnp.float32)
        logits = jnp.where((dist >= 0)[None, None, None], logits, -jnp.inf)
        p = jax.nn.softmax(logits, axis=-1)
        attn = p[:, :, 0] - lam * p[:, :, 1]
        return jnp.einsum('bhqk,bhke->bhqe', attn.astype(v.dtype), v)

    o = lax.map(block_fn, (q_blocks, jnp.arange(nq, dtype=jnp.int32)))
    o = jnp.moveaxis(o, 0, 2).reshape(B, H, S, DA_VDIM)
    o = rms_norm(o, subln_g) * (1.0 - lam_init)
    o = o.transpose(0, 2, 1, 3).reshape(B, S, H * DA_VDIM)
    return o @ w_out


def spatial_gating(x, w_in, b_in, ln_g, ln_b, w_s, b_s, w_out):
    B, S, _ = x.shape
    z = jax.nn.gelu(x @ w_in + b_in)
    u, v = z[..., :SG_HALF], z[..., SG_HALF:]
    v = layer_norm(v, ln_g, ln_b)
    nc = S // SG_CHUNK
    v = v.reshape(B, nc, SG_CHUNK, SG_GROUPS, SG_GROUP_DIM)
    causal = jnp.tril(jnp.ones((SG_CHUNK, SG_CHUNK), dtype=bool))
    w = jnp.where(causal[None], w_s, jnp.zeros_like(w_s))
    s = jnp.einsum('gts,bnsgc->bntgc', w, v) + jnp.transpose(b_s)[:, :, None]
    s = s.reshape(B, S, SG_HALF)
    return (u * s) @ w_out


def hier_moe(x, w_group, b_group, w_router, b_router, w_gate, w_up, w_down):
    B, S, D = x.shape
    N = B * S
    xt = x.reshape(N, D)
    g_logits = (xt @ w_group).astype(jnp.float32) + b_group.astype(jnp.float32)
    g_prob = jax.nn.softmax(g_logits, axis=-1)
    g_idx = jnp.argmax(g_logits, axis=-1).astype(jnp.int32)
    g_gate = jnp.take_along_axis(g_prob, g_idx[:, None], axis=-1)
    e_logits = ((xt @ w_router).astype(jnp.float32) + b_router.astype(jnp.float32))
    e_logits = e_logits.reshape(N, MOE_GROUPS, MOE_PER_GROUP)
    e_logits = jnp.take_along_axis(e_logits, g_idx[:, None, None], axis=1)[:, 0]
    top_val, top_idx = lax.top_k(e_logits, MOE_TOPK)
    gates = g_gate * jax.nn.softmax(top_val, axis=-1)
    expert = g_idx[:, None] * MOE_PER_GROUP + top_idx.astype(jnp.int32)

    A = N * MOE_TOPK
    a_exp = expert.reshape(A)
    a_tok = jnp.arange(A, dtype=jnp.int32) // MOE_TOPK
    a_gate = gates.reshape(A)
    order = jnp.argsort(a_exp)
    s_exp = a_exp[order]
    counts = jnp.zeros((MOE_EXPERTS,), jnp.int32).at[a_exp].add(1)
    padded = (counts + MOE_BLOCK - 1) // MOE_BLOCK * MOE_BLOCK
    starts = jnp.cumsum(counts) - counts
    pends = jnp.cumsum(padded)
    pstarts = pends - padded
    dest = pstarts[s_exp] + jnp.arange(A, dtype=jnp.int32) - starts[s_exp]
    n_blocks = -(-A // MOE_BLOCK) + MOE_EXPERTS
    P = n_blocks * MOE_BLOCK
    row_tok = jnp.full((P,), N, jnp.int32).at[dest].set(a_tok[order])
    row_gate = jnp.zeros((P,), jnp.float32).at[dest].set(a_gate[order])
    block_start = jnp.arange(n_blocks, dtype=jnp.int32) * MOE_BLOCK
    block_exp = jnp.minimum(jnp.searchsorted(pends, block_start, side='right'),
                            MOE_EXPERTS - 1).astype(jnp.int32)
    x_pad = jnp.concatenate([xt, jnp.zeros((1, D), xt.dtype)], axis=0)
    rows = x_pad[row_tok].reshape(n_blocks, MOE_BLOCK, D)

    def expert_block(args):
        r, e = args
        h = jax.nn.silu(r @ w_gate[e]) * (r @ w_up[e])
        return h @ w_down[e]

    y = lax.map(expert_block, (rows, block_exp)).reshape(P, D)
    out = jnp.zeros((N + 1, D), y.dtype).at[row_tok].add(y * row_gate[:, None].astype(y.dtype))
    return out[:N].reshape(B, S, D)


def setup_inputs(seed: int = 0) -> dict:
    key = jax.random.key(seed)
    ks = jax.random.split(key, 24)
    f32 = jnp.float32
    nrm = lambda k, shape, s: jax.random.normal(k, shape, f32) * s
    D, H, d = D_MODEL, DA_HEADS, DA_HEAD_DIM
    NA, NB, L = N_ATTN_LAYERS, N_SG_LAYERS, DEPTH
    return {
        "x": nrm(ks[0], (BATCH, SEQ, D), 1.0),
        "rel_bias": nrm(ks[1], (REL_BUCKETS, H), 0.5),
        "attn_w_in": nrm(ks[2], (NA, D, 2 * H * 2 * d + H * DA_VDIM), D ** -0.5),
        "attn_w_out": nrm(ks[3], (NA, H * DA_VDIM, D), (H * DA_VDIM) ** -0.5 * DN_BETA),
        "attn_lambda": nrm(ks[4], (NA, 4, d), 0.1),
        "attn_subln_g": 1.0 + nrm(ks[5], (NA, DA_VDIM), 0.02),
        "sg_w_in": nrm(ks[6], (NB, D, 2 * SG_HALF), D ** -0.5),
        "sg_b_in": nrm(ks[7], (NB, 2 * SG_HALF), 0.02),
        "sg_ln_g": 1.0 + nrm(ks[8], (NB, SG_HALF), 0.02),
        "sg_ln_b": nrm(ks[9], (NB, SG_HALF), 0.02),
        "sg_w_s": nrm(ks[10], (NB, SG_GROUPS, SG_CHUNK, SG_CHUNK), SG_CHUNK ** -0.5),
        "sg_b_s": 1.0 + nrm(ks[11], (NB, SG_GROUPS, SG_CHUNK), 0.1),
        "sg_w_out": nrm(ks[12], (NB, SG_HALF, D), SG_HALF ** -0.5 * DN_BETA),
        "moe_w_group": nrm(ks[13], (L, D, MOE_GROUPS), D ** -0.5),
        "moe_b_group": nrm(ks[14], (L, MOE_GROUPS), 0.01),
        "moe_w_router": nrm(ks[15], (L, D, MOE_EXPERTS), D ** -0.5),
        "moe_b_router": nrm(ks[16], (L, MOE_EXPERTS), 0.01),
        "moe_w_gate": nrm(ks[17], (L, MOE_EXPERTS, D, MOE_HIDDEN), D ** -0.5),
        "moe_w_up": nrm(ks[18], (L, MOE_EXPERTS, D, MOE_HIDDEN), D ** -0.5),
        "moe_w_down": nrm(ks[19], (L, MOE_EXPERTS, MOE_HIDDEN, D), MOE_HIDDEN ** -0.5 * DN_BETA),
        "ln_g": 1.0 + nrm(ks[20], (L, 2, D), 0.02),
        "ln_b": nrm(ks[21], (L, 2, D), 0.02),
    }


def reference(x, rel_bias, attn_w_in, attn_w_out, attn_lambda, attn_subln_g,
              sg_w_in, sg_b_in, sg_ln_g, sg_ln_b, sg_w_s, sg_b_s, sg_w_out,
              moe_w_group, moe_b_group, moe_w_router, moe_b_router,
              moe_w_gate, moe_w_up, moe_w_down, ln_g, ln_b):
    h = x
    for i in range(DEPTH):
        j = i // N_MIXERS
        if i % N_MIXERS == 0:
            lam_init = 0.8 - 0.6 * math.exp(-0.3 * i)
            mix = diff_attention(h, attn_w_in[j], attn_w_out[j], attn_lambda[j],
                                 attn_subln_g[j], rel_bias, lam_init)
        else:
            mix = spatial_gating(h, sg_w_in[j], sg_b_in[j], sg_ln_g[j], sg_ln_b[j],
                                 sg_w_s[j], sg_b_s[j], sg_w_out[j])
        h = layer_norm(DN_ALPHA * h + mix, ln_g[i, 0], ln_b[i, 0])
        ffn = hier_moe(h, moe_w_group[i], moe_b_group[i], moe_w_router[i], moe_b_router[i],
                       moe_w_gate[i], moe_w_up[i], moe_w_down[i])
        h = layer_norm(DN_ALPHA * h + ffn, ln_g[i, 1], ln_b[i, 1])
    return h
```

```python
import functools
import math

import numpy as np
import jax
import jax.numpy as jnp
from jax import lax
from jax.experimental import pallas as pl
from jax.experimental.pallas import tpu as pltpu

F32 = jnp.float32
BF16 = jnp.bfloat16
I32 = jnp.int32

DA_HEADS = 8
DA_HEAD_DIM = 64
DA_VDIM = 2 * DA_HEAD_DIM
REL_BUCKETS = 32
REL_MAX_DIST = 128
SG_CHUNK = 128
SG_GROUPS = 8
MOE_GROUPS = 8
MOE_PER_GROUP = 8
MOE_EXPERTS = MOE_GROUPS * MOE_PER_GROUP
LN_EPS = 1e-5
DEPTH = 2
DN_ALPHA = (2 * DEPTH) ** 0.25

LANES = 128
NEG_BIG = -1e30
VMEM_LIMIT = 52 * 1024 * 1024

ROW_TILE = 512
ATTN_TILE = 256
SG_TILE = 2 * SG_CHUNK
MOE_TILE = 256
COMBINE_TILE = 256


def _params(sem):
    return pltpu.CompilerParams(dimension_semantics=sem, vmem_limit_bytes=VMEM_LIMIT)


def _layer_norm(y, g, b):
    mu = jnp.mean(y, axis=-1, keepdims=True)
    yc = y - mu
    var = jnp.mean(yc * yc, axis=-1, keepdims=True)
    return yc * lax.rsqrt(var + LN_EPS) * g + b


def _qkv_body(x_ref, w_ref, o_ref, *, d_model, q_scale):
    x = x_ref[...].astype(BF16)
    for c in range(3):
        cols = slice(c * d_model, (c + 1) * d_model)
        acc = jnp.dot(x, w_ref[:, cols], preferred_element_type=F32)
        if c == 0:
            acc = acc * q_scale
        o_ref[:, cols] = acc.astype(BF16)


def _qkv_proj(x2d, w_bf16):
    n, d = x2d.shape
    tm = min(ROW_TILE, n)
    return pl.pallas_call(
        functools.partial(_qkv_body, d_model=d, q_scale=DA_HEAD_DIM ** -0.5),
        out_shape=jax.ShapeDtypeStruct((n, 3 * d), BF16),
        grid=(n // tm,),
        in_specs=[pl.BlockSpec((tm, d), lambda i: (i, 0)),
                  pl.BlockSpec((d, 3 * d), lambda i: (0, 0))],
        out_specs=pl.BlockSpec((tm, 3 * d), lambda i: (i, 0)),
        compiler_params=_params(("parallel",)),
        name="qkv_proj",
    )(x2d, w_bf16)


def _t5_bucket_np(dist):
    n = np.maximum(dist, 0)
    max_exact = REL_BUCKETS // 2
    nf = np.maximum(n, 1).astype(np.float64)
    large = max_exact + (np.log(nf / max_exact) / math.log(REL_MAX_DIST / max_exact)
                         * (REL_BUCKETS - max_exact)).astype(np.int64)
    large = np.minimum(large, REL_BUCKETS - 1)
    return np.where(n < max_exact, n, large)


def _bucket_tiles(t, seq):
    assert np.all(_t5_bucket_np(np.arange(t + 1, max(seq, t + 2))) == REL_BUCKETS - 1)
    a = np.arange(t)[:, None]
    b = np.arange(t)[None, :]
    diag = np.where(a >= b, _t5_bucket_np(a - b), -1)
    sub = _t5_bucket_np(t + a - b)
    return np.stack([diag, sub]).astype(np.int32)


def _bias_body(tbl_ref, bucket_ref, o_ref):
    h = pl.program_id(0)
    last = tbl_ref[REL_BUCKETS - 1, h]
    bk = bucket_ref[...]
    acc = jnp.zeros(bk.shape, F32)
    for b in range(REL_BUCKETS - 1):
        acc = jnp.where(bk == b, tbl_ref[b, h] - last, acc)
    o_ref[0] = jnp.where(bk < 0, NEG_BIG, acc)


def _bias_tiles(rel_bias, t, seq):
    heads = rel_bias.shape[1]
    buckets = jnp.asarray(_bucket_tiles(t, seq))
    return pl.pallas_call(
        _bias_body,
        out_shape=jax.ShapeDtypeStruct((heads, 2, t, t), F32),
        grid=(heads,),
        in_specs=[pl.BlockSpec(memory_space=pltpu.SMEM),
                  pl.BlockSpec((2, t, t), lambda h: (0, 0, 0))],
        out_specs=pl.BlockSpec((1, 2, t, t), lambda h: (h, 0, 0, 0)),
        compiler_params=_params(("arbitrary",)),
        name="rel_bias_tiles",
    )(rel_bias, buckets)


def _attn_body(q_ref, k_ref, v_ref, bias_ref, lam_ref, g_ref, o_ref,
               m_sc, l_sc, acc_sc, *, t, lam_init):
    i = pl.program_id(2)
    q = q_ref[0]
    lane = lax.broadcasted_iota(I32, q.shape, 1)
    zero = jnp.zeros_like(q)
    q2 = jnp.concatenate([jnp.where(lane < DA_HEAD_DIM, q, zero),
                          jnp.where(lane >= DA_HEAD_DIM, q, zero)], axis=0)

    m_sc[...] = jnp.full(m_sc.shape, NEG_BIG, F32)
    l_sc[...] = jnp.zeros(l_sc.shape, F32)
    acc_sc[...] = jnp.zeros(acc_sc.shape, F32)

    def block(j, bias):
        start = pl.multiple_of(j * t, t)
        k = k_ref[0, pl.ds(start, t), :]
        v = v_ref[0, pl.ds(start, t), :]
        s = lax.dot_general(q2, k, (((1,), (1,)), ((), ())),
                            preferred_element_type=F32)
        if bias is not None:
            s = (s.reshape(2, t, t) + bias[None]).reshape(2 * t, t)
        m_prev = m_sc[...]
        m_new = jnp.maximum(m_prev, jnp.max(s, axis=-1, keepdims=True))
        alpha = jnp.exp(m_prev - m_new)
        p = jnp.exp(s - m_new)
        l_sc[...] = alpha * l_sc[...] + jnp.sum(p, axis=-1, keepdims=True)
        acc_sc[...] = alpha * acc_sc[...] + jnp.dot(p.astype(BF16), v,
                                                    preferred_element_type=F32)
        m_sc[...] = m_new

    def far_block(j, carry):
        block(j, None)
        return carry

    lax.fori_loop(0, jnp.maximum(i - 1, 0), far_block, 0)

    @pl.when(i >= 1)
    def _():
        block(i - 1, bias_ref[0, 1])

    block(i, bias_ref[0, 0])

    lv = lam_ref[...]
    dot_a = jnp.sum(lv[0:1] * lv[1:2], axis=-1, keepdims=True)
    dot_b = jnp.sum(lv[2:3] * lv[3:4], axis=-1, keepdims=True)
    lam = jnp.exp(dot_a) - jnp.exp(dot_b) + lam_init

    o_all = acc_sc[...] / l_sc[...]
    o = o_all[:t] - lam * o_all[t:]
    ms = jnp.mean(o * o, axis=-1, keepdims=True)
    o = o * lax.rsqrt(ms + LN_EPS) * g_ref[...] * (1.0 - lam_init)
    o_ref[0] = o.astype(o_ref.dtype)


def _diff_attention(qkv, bias_tiles, lam_vecs, subln_g, lam_init, t):
    b, s, d3 = qkv.shape
    d = d3 // 3
    heads = DA_HEADS
    assert d == heads * DA_VDIM and DA_VDIM == LANES and s % t == 0
    return pl.pallas_call(
        functools.partial(_attn_body, t=t, lam_init=lam_init),
        out_shape=jax.ShapeDtypeStruct((b, s, d), BF16),
        grid=(b, heads, s // t),
        in_specs=[
            pl.BlockSpec((1, t, LANES), lambda bi, h, i: (bi, i, h)),
            pl.BlockSpec((1, s, LANES), lambda bi, h, i: (bi, 0, heads + h)),
            pl.BlockSpec((1, s, LANES), lambda bi, h, i: (bi, 0, 2 * heads + h)),
            pl.BlockSpec((1, 2, t, t), lambda bi, h, i: (h, 0, 0, 0)),
            pl.BlockSpec((4, DA_HEAD_DIM), lambda bi, h, i: (0, 0)),
            pl.BlockSpec((1, DA_VDIM), lambda bi, h, i: (0, 0)),
        ],
        out_specs=pl.BlockSpec((1, t, LANES), lambda bi, h, i: (bi, i, h)),
        scratch_shapes=[pltpu.VMEM((2 * t, 1), F32),
                        pltpu.VMEM((2 * t, 1), F32),
                        pltpu.VMEM((2 * t, DA_VDIM), F32)],
        compiler_params=_params(("parallel", "parallel", "arbitrary")),
        name="diff_attention",
    )(qkv, qkv, qkv, bias_tiles, lam_vecs, subln_g.reshape(1, DA_VDIM))


def _route(logits):
    lane = lax.broadcasted_iota(I32, logits.shape, 1)
    lane_f = lane.astype(F32)
    far = float(2 * LANES)
    is_g = lane < MOE_GROUPS
    gl = jnp.where(is_g, logits, NEG_BIG)
    gmax = jnp.max(gl, axis=-1, keepdims=True)
    gidx = jnp.min(jnp.where(gl == gmax, lane_f, far), axis=-1, keepdims=True)
    gsum = jnp.sum(jnp.where(is_g, jnp.exp(gl - gmax), 0.0), axis=-1, keepdims=True)
    g_gate = 1.0 / gsum
    lo = MOE_GROUPS + MOE_PER_GROUP * gidx
    in_grp = (lane_f >= lo) & (lane_f < lo + MOE_PER_GROUP)
    el = jnp.where(in_grp, logits, NEG_BIG)
    v1 = jnp.max(el, axis=-1, keepdims=True)
    i1 = jnp.min(jnp.where(el == v1, lane_f, far), axis=-1, keepdims=True)
    el2 = jnp.where(lane_f == i1, NEG_BIG, el)
    v2 = jnp.max(el2, axis=-1, keepdims=True)
    i2 = jnp.min(jnp.where(el2 == v2, lane_f, far), axis=-1, keepdims=True)
    e21 = jnp.exp(v2 - v1)
    w1 = g_gate / (1.0 + e21)
    w2 = g_gate * e21 / (1.0 + e21)
    out = jnp.where(lane == 0, i1 - MOE_GROUPS, 0.0)
    out = jnp.where(lane == 1, i2 - MOE_GROUPS, out)
    out = jnp.where(lane == 2, w1, out)
    out = jnp.where(lane == 3, w2, out)
    return out


def _proj_ln_body(a_ref, w_ref, h_ref, g_ref, b_ref, wr_ref, br_ref, hout_ref, route_ref):
    mix = jnp.dot(a_ref[...], w_ref[...], preferred_element_type=F32)
    hn = _layer_norm(DN_ALPHA * h_ref[...] + mix, g_ref[...], b_ref[...])
    hout_ref[...] = hn
    logits = jnp.dot(hn, wr_ref[...], precision=lax.Precision.HIGHEST,
                     preferred_element_type=F32) + br_ref[...]
    route_ref[...] = _route(logits)


def _proj_ln_route(a, w_bf16, h, ln_g, ln_b, w_route, b_route):
    n, k = a.shape
    d = w_bf16.shape[1]
    tm = min(ROW_TILE, n)
    const = lambda i: (0, 0)
    return pl.pallas_call(
        _proj_ln_body,
        out_shape=(jax.ShapeDtypeStruct((n, d), F32),
                   jax.ShapeDtypeStruct((n, LANES), F32)),
        grid=(n // tm,),
        in_specs=[pl.BlockSpec((tm, k), lambda i: (i, 0)),
                  pl.BlockSpec((k, d), const),
                  pl.BlockSpec((tm, d), lambda i: (i, 0)),
                  pl.BlockSpec((1, d), const),
                  pl.BlockSpec((1, d), const),
                  pl.BlockSpec((d, LANES), const),
                  pl.BlockSpec((1, LANES), const)],
        out_specs=(pl.BlockSpec((tm, d), lambda i: (i, 0)),
                   pl.BlockSpec((tm, LANES), lambda i: (i, 0))),
        compiler_params=_params(("parallel",)),
        name="proj_ln_route",
    )(a, w_bf16, h, ln_g.reshape(1, d), ln_b.reshape(1, d), w_route, b_route)


def _router_params(w_group, b_group, w_router, b_router):
    d = w_group.shape[0]
    pad = LANES - MOE_GROUPS - MOE_EXPERTS
    w = jnp.concatenate([w_group, w_router, jnp.zeros((d, pad), F32)], axis=1)
    b = jnp.concatenate([b_group, b_router, jnp.zeros((pad,), F32)]).reshape(1, LANES)
    return w, b


def _moe_meta(e_ids, tm):
    n = e_ids.shape[0]
    a = 2 * n
    a_exp = e_ids.reshape(a)
    onehot = (a_exp[:, None] == jnp.arange(MOE_EXPERTS, dtype=I32)[None, :]).astype(I32)
    csum = jnp.cumsum(onehot, axis=0)
    rank = jnp.take_along_axis(csum, a_exp[:, None], axis=1)[:, 0] - 1
    counts = csum[-1]
    padded = (counts + tm - 1) // tm * tm
    pends = jnp.cumsum(padded)
    pstarts = pends - padded
    dest = pstarts[a_exp] + rank
    nb = a // tm + MOE_EXPERTS
    row_tok = jnp.zeros((nb * tm,), I32).at[dest].set(jnp.arange(a, dtype=I32) // 2)
    nvalid = (pends[-1] // tm).astype(I32)
    bstart = jnp.arange(nb, dtype=I32) * tm
    block_exp = jnp.minimum(jnp.searchsorted(pends, bstart, side="right"),
                            MOE_EXPERTS - 1).astype(I32)
    block_exp = jnp.where(jnp.arange(nb) < nvalid, block_exp, block_exp[nvalid - 1])
    return row_tok.reshape(nb, 1, tm), block_exp, nvalid.reshape(1), dest.reshape(n, 2)


def _row_gather_start(idx_ref, n_rows, src_hbm, dst_buf, slot, sem):
    def body(r, carry):
        row = idx_ref[0, 0, r]
        pltpu.make_async_copy(src_hbm.at[pl.ds(row, 1), :],
                              dst_buf.at[slot, pl.ds(r, 1), :],
                              sem.at[slot]).start()
        return carry
    lax.fori_loop(0, n_rows, body, 0, unroll=8)


def _row_gather_wait(n_rows, src_hbm, dst_buf, slot, sem):
    pltpu.make_async_copy(src_hbm.at[pl.ds(0, n_rows), :], dst_buf.at[slot],
                          sem.at[slot]).wait()


def _gmm_body(be_ref, nv_ref, idx_cur_ref, idx_nxt_ref, x_hbm, wg_ref, wu_ref, wd_ref,
              y_ref, xbuf, sem, *, tm):
    i = pl.program_id(0)
    nv = nv_ref[0]

    @pl.when(i == 0)
    def _():
        _row_gather_start(idx_cur_ref, tm, x_hbm, xbuf, 0, sem)

    @pl.when(i + 1 < nv)
    def _():
        _row_gather_start(idx_nxt_ref, tm, x_hbm, xbuf, (i + 1) % 2, sem)

    @pl.when(i < nv)
    def _():
        slot = i % 2
        _row_gather_wait(tm, x_hbm, xbuf, slot, sem)
        x = xbuf[slot].astype(BF16)
        gate = jnp.dot(x, wg_ref[0], preferred_element_type=F32)
        up = jnp.dot(x, wu_ref[0], preferred_element_type=F32)
        hid = (gate / (1.0 + jnp.exp(-gate)) * up).astype(BF16)
        y_ref[...] = jnp.dot(hid, wd_ref[0], preferred_element_type=F32)

    @pl.when(i >= nv)
    def _():
        y_ref[...] = jnp.zeros(y_ref.shape, y_ref.dtype)


def _grouped_experts(x, row_tok, block_exp, nvalid, wg, wu, wd, tm):
    n, d = x.shape
    nb = row_tok.shape[0]
    hd = wg.shape[2]
    clamp = lambda i, nv: jnp.minimum(i, nv[0] - 1)
    grid_spec = pltpu.PrefetchScalarGridSpec(
        num_scalar_prefetch=2,
        grid=(nb,),
        in_specs=[
            pl.BlockSpec((1, 1, tm), lambda i, be, nv: (clamp(i, nv), 0, 0),
                         memory_space=pltpu.SMEM),
            pl.BlockSpec((1, 1, tm), lambda i, be, nv: (clamp(i + 1, nv), 0, 0),
                         memory_space=pltpu.SMEM),
            pl.BlockSpec(memory_space=pl.ANY),
            pl.BlockSpec((1, d, hd), lambda i, be, nv: (be[i], 0, 0)),
            pl.BlockSpec((1, d, hd), lambda i, be, nv: (be[i], 0, 0)),
            pl.BlockSpec((1, hd, d), lambda i, be, nv: (be[i], 0, 0)),
        ],
        out_specs=pl.BlockSpec((tm, d), lambda i, be, nv: (i, 0)),
        scratch_shapes=[pltpu.VMEM((2, tm, d), x.dtype),
                        pltpu.SemaphoreType.DMA((2,))],
    )
    return pl.pallas_call(
        functools.partial(_gmm_body, tm=tm),
        out_shape=jax.ShapeDtypeStruct((nb * tm, d), F32),
        grid_spec=grid_spec,
        compiler_params=_params(("arbitrary",)),
        name="grouped_experts",
    )(block_exp, nvalid, row_tok, row_tok, x, wg, wu, wd)


def _combine_body(pos_cur_ref, pos_nxt_ref, y_hbm, h_ref, route_ref, g_ref, b_ref,
                  *rest, tm, n_tiles, with_bf16):
    if with_bf16:
        out_ref, out_bf_ref, ybuf, sem = rest
    else:
        out_ref, ybuf, sem = rest
    i = pl.program_id(0)

    @pl.when(i == 0)
    def _():
        _row_gather_start(pos_cur_ref, 2 * tm, y_hbm, ybuf, 0, sem)

    @pl.when(i + 1 < n_tiles)
    def _():
        _row_gather_start(pos_nxt_ref, 2 * tm, y_hbm, ybuf, (i + 1) % 2, sem)

    slot = i % 2
    _row_gather_wait(2 * tm, y_hbm, ybuf, slot, sem)
    y1 = ybuf[slot, 0:tm, :]
    y2 = ybuf[slot, tm:2 * tm, :]
    route = route_ref[...]
    ffn = route[:, 2:3] * y1 + route[:, 3:4] * y2
    hn = _layer_norm(DN_ALPHA * h_ref[...] + ffn, g_ref[...], b_ref[...])
    out_ref[...] = hn
    if with_bf16:
        out_bf_ref[...] = hn.astype(BF16)


def _combine_ln(y_sorted, pos, h, route, ln_g, ln_b, with_bf16):
    n, d = h.shape
    tm = min(COMBINE_TILE, n)
    n_tiles = n // tm
    pos_tiles = pos.reshape(n_tiles, tm, 2).transpose(0, 2, 1).reshape(n_tiles, 1, 2 * tm)
    const = lambda i: (0, 0)
    out_shape = [jax.ShapeDtypeStruct((n, d), F32)]
    out_specs = [pl.BlockSpec((tm, d), lambda i: (i, 0))]
    if with_bf16:
        out_shape.append(jax.ShapeDtypeStruct((n, d), BF16))
        out_specs.append(pl.BlockSpec((tm, d), lambda i: (i, 0)))
    return pl.pallas_call(
        functools.partial(_combine_body, tm=tm, n_tiles=n_tiles, with_bf16=with_bf16),
        out_shape=tuple(out_shape),
        grid=(n_tiles,),
        in_specs=[
            pl.BlockSpec((1, 1, 2 * tm), lambda i: (i, 0, 0), memory_space=pltpu.SMEM),
            pl.BlockSpec((1, 1, 2 * tm), lambda i: (jnp.minimum(i + 1, n_tiles - 1), 0, 0),
                         memory_space=pltpu.SMEM),
            pl.BlockSpec(memory_space=pl.ANY),
            pl.BlockSpec((tm, d), lambda i: (i, 0)),
            pl.BlockSpec((tm, LANES), lambda i: (i, 0)),
            pl.BlockSpec((1, d), const),
            pl.BlockSpec((1, d), const),
        ],
        out_specs=tuple(out_specs),
        scratch_shapes=[pltpu.VMEM((2, 2 * tm, d), y_sorted.dtype),
                        pltpu.SemaphoreType.DMA((2,))],
        compiler_params=_params(("arbitrary",)),
        name="moe_combine_ln",
    )(pos_tiles, pos_tiles, y_sorted, h, route, ln_g.reshape(1, d), ln_b.reshape(1, d))


def _hier_moe_ln(h, route, w_gate, w_up, w_down, ln_g, ln_b, with_bf16):
    e_ids = route[:, 0:2].astype(I32)
    row_tok, block_exp, nvalid, pos = _moe_meta(e_ids, MOE_TILE)
    y_sorted = _grouped_experts(h, row_tok, block_exp, nvalid,
                                w_gate.astype(BF16), w_up.astype(BF16),
                                w_down.astype(BF16), MOE_TILE)
    return _combine_ln(y_sorted, pos, h, route, ln_g, ln_b, with_bf16)


def _gelu_tanh(x):
    c = math.sqrt(2.0 / math.pi)
    return 0.5 * x * (1.0 + jnp.tanh(c * (x + 0.044715 * (x * x * x))))


def _sg_body(x_ref, w_ref, b_ref, lng_ref, lnb_ref, ws_ref, bs_ref, o_ref, *, half, gdim):
    x = x_ref[...]
    v = _gelu_tanh(jnp.dot(x, w_ref[:, half:], preferred_element_type=F32) + b_ref[:, half:])
    vn = _layer_norm(v, lng_ref[...], lnb_ref[...]).astype(BF16)
    for g in range(SG_GROUPS):
        cols = slice(g * gdim, (g + 1) * gdim)
        u = _gelu_tanh(jnp.dot(x, w_ref[:, cols], preferred_element_type=F32) + b_ref[:, cols])
        s = jnp.dot(ws_ref[g], vn[:, cols], preferred_element_type=F32) + bs_ref[:, g:g + 1]
        o_ref[:, cols] = (u * s).astype(o_ref.dtype)


def _spatial_gating(x_bf16, w_in, b_in, ln_g, ln_b, w_s, b_s):
    n, d = x_bf16.shape
    half = w_in.shape[1] // 2
    gdim = half // SG_GROUPS
    tm = SG_TILE
    reps = tm // SG_CHUNK
    causal = np.tril(np.ones((SG_CHUNK, SG_CHUNK), dtype=bool))
    w_tri = jnp.where(causal[None], w_s, 0.0).astype(BF16)
    eye = jnp.eye(reps, dtype=BF16)
    w_blk = jnp.einsum("ab,gts->gatbs", eye, w_tri).reshape(SG_GROUPS, tm, tm)
    bs_tile = jnp.tile(jnp.transpose(b_s), (reps, 1))
    const2 = lambda i: (0, 0)
    return pl.pallas_call(
        functools.partial(_sg_body, half=half, gdim=gdim),
        out_shape=jax.ShapeDtypeStruct((n, half), BF16),
        grid=(n // tm,),
        in_specs=[pl.BlockSpec((tm, d), lambda i: (i, 0)),
                  pl.BlockSpec((d, 2 * half), const2),
                  pl.BlockSpec((1, 2 * half), const2),
                  pl.BlockSpec((1, half), const2),
                  pl.BlockSpec((1, half), const2),
                  pl.BlockSpec((SG_GROUPS, tm, tm), lambda i: (0, 0, 0)),
                  pl.BlockSpec((tm, SG_GROUPS), const2)],
        out_specs=pl.BlockSpec((tm, half), lambda i: (i, 0)),
        compiler_params=_params(("parallel",)),
        name="spatial_gating",
    )(x_bf16, w_in.astype(BF16), b_in.reshape(1, -1), ln_g.reshape(1, -1),
      ln_b.reshape(1, -1), w_blk, bs_tile)


def kernel(x, rel_bias, attn_w_in, attn_w_out, attn_lambda, attn_subln_g, sg_w_in, sg_b_in,
           sg_ln_g, sg_ln_b, sg_w_s, sg_b_s, sg_w_out, moe_w_group, moe_b_group, moe_w_router,
           moe_b_router, moe_w_gate, moe_w_up, moe_w_down, ln_g, ln_b):
    b, s, d = x.shape
    n = b * s
    h = x.reshape(n, d)

    lam_init = 0.8 - 0.6 * math.exp(-0.3 * 0)
    t = min(ATTN_TILE, s)
    qkv = _qkv_proj(h, attn_w_in[0].astype(BF16)).reshape(b, s, 3 * d)
    bias_tiles = _bias_tiles(rel_bias, t, s)
    o = _diff_attention(qkv, bias_tiles, attn_lambda[0], attn_subln_g[0], lam_init, t)
    wr, br = _router_params(moe_w_group[0], moe_b_group[0], moe_w_router[0], moe_b_router[0])
    h, route = _proj_ln_route(o.reshape(n, d), attn_w_out[0].astype(BF16), h,
                              ln_g[0, 0], ln_b[0, 0], wr, br)
    h, h_bf16 = _hier_moe_ln(h, route, moe_w_gate[0], moe_w_up[0], moe_w_down[0],
                             ln_g[0, 1], ln_b[0, 1], with_bf16=True)

    gated = _spatial_gating(h_bf16, sg_w_in[0], sg_b_in[0], sg_ln_g[0], sg_ln_b[0],
                            sg_w_s[0], sg_b_s[0])
    wr, br = _router_params(moe_w_group[1], moe_b_group[1], moe_w_router[1], moe_b_router[1])
    h, route = _proj_ln_route(gated, sg_w_out[0].astype(BF16), h,
                              ln_g[1, 0], ln_b[1, 0], wr, br)
    (h,) = _hier_moe_ln(h, route, moe_w_gate[1], moe_w_up[1], moe_w_down[1],
                        ln_g[1, 1], ln_b[1, 1], with_bf16=False)
    return h.reshape(b, s, d)
```

```python
import functools
import math

import numpy as np
import jax
import jax.numpy as jnp
from jax import lax
from jax.experimental import pallas as pl
from jax.experimental.pallas import tpu as pltpu

F32 = jnp.float32
BF16 = jnp.bfloat16
I32 = jnp.int32

DA_HEADS = 8
DA_HEAD_DIM = 64
DA_VDIM = 2 * DA_HEAD_DIM
REL_BUCKETS = 32
REL_MAX_DIST = 128
SG_CHUNK = 128
SG_GROUPS = 8
MOE_GROUPS = 8
MOE_PER_GROUP = 8
MOE_EXPERTS = MOE_GROUPS * MOE_PER_GROUP
LN_EPS = 1e-5
DEPTH = 2
DN_ALPHA = (2 * DEPTH) ** 0.25

LANES = 128
NEG_BIG = -1e30
LOG2E = math.log2(math.e)
VMEM_LIMIT = 52 * 1024 * 1024

ROW_TILE = 512
ATTN_TILE = 512
SG_TILE = 2 * SG_CHUNK
MOE_TILE = 256
COMBINE_TILE = 256


def _params(sem):
    return pltpu.CompilerParams(dimension_semantics=sem, vmem_limit_bytes=VMEM_LIMIT)


def _layer_norm(y, g, b):
    mu = jnp.mean(y, axis=-1, keepdims=True)
    yc = y - mu
    var = jnp.mean(yc * yc, axis=-1, keepdims=True)
    return yc * lax.rsqrt(var + LN_EPS) * g + b


def _qkv_body(x_ref, w_ref, o_ref, *, d_model, q_scale):
    x = x_ref[...].astype(BF16)
    for c in range(3):
        cols = slice(c * d_model, (c + 1) * d_model)
        acc = jnp.dot(x, w_ref[:, cols], preferred_element_type=F32)
        if c == 0:
            acc = acc * q_scale
        o_ref[:, cols] = acc.astype(BF16)


def _qkv_proj(x2d, w_bf16):
    n, d = x2d.shape
    tm = min(ROW_TILE, n)
    return pl.pallas_call(
        functools.partial(_qkv_body, d_model=d, q_scale=DA_HEAD_DIM ** -0.5 * LOG2E),
        out_shape=jax.ShapeDtypeStruct((n, 3 * d), BF16),
        grid=(n // tm,),
        in_specs=[pl.BlockSpec((tm, d), lambda i: (i, 0)),
                  pl.BlockSpec((d, 3 * d), lambda i: (0, 0))],
        out_specs=pl.BlockSpec((tm, 3 * d), lambda i: (i, 0)),
        compiler_params=_params(("parallel",)),
        name="qkv_proj",
    )(x2d, w_bf16)


def _t5_bucket_np(dist):
    n = np.maximum(dist, 0)
    max_exact = REL_BUCKETS // 2
    nf = np.maximum(n, 1).astype(np.float64)
    large = max_exact + (np.log(nf / max_exact) / math.log(REL_MAX_DIST / max_exact)
                         * (REL_BUCKETS - max_exact)).astype(np.int64)
    large = np.minimum(large, REL_BUCKETS - 1)
    return np.where(n < max_exact, n, large)


def _bucket_tiles(t, seq):
    assert np.all(_t5_bucket_np(np.arange(t + 1, max(seq, t + 2))) == REL_BUCKETS - 1)
    kk = np.arange(t)[:, None]
    qq = np.arange(t)[None, :]
    diag = np.where(qq >= kk, _t5_bucket_np(qq - kk), -1)
    sub = _t5_bucket_np(t + qq - kk)
    return np.stack([diag, sub]).astype(np.int32)


def _bias_body(tbl_ref, bucket_ref, o_ref):
    h = pl.program_id(0)
    last = tbl_ref[REL_BUCKETS - 1, h]
    bk = bucket_ref[...]
    acc = jnp.zeros(bk.shape, F32)
    for b in range(REL_BUCKETS - 1):
        acc = jnp.where(bk == b, (tbl_ref[b, h] - last) * LOG2E, acc)
    o_ref[0] = jnp.where(bk < 0, NEG_BIG, acc)


def _bias_tiles(rel_bias, t, seq):
    heads = rel_bias.shape[1]
    buckets = jnp.asarray(_bucket_tiles(t, seq))
    return pl.pallas_call(
        _bias_body,
        out_shape=jax.ShapeDtypeStruct((heads, 2, t, t), F32),
        grid=(heads,),
        in_specs=[pl.BlockSpec(memory_space=pltpu.SMEM),
                  pl.BlockSpec((2, t, t), lambda h: (0, 0, 0))],
        out_specs=pl.BlockSpec((1, 2, t, t), lambda h: (h, 0, 0, 0)),
        compiler_params=_params(("arbitrary",)),
        name="rel_bias_tiles",
    )(rel_bias, buckets)


def _attn_body(q_ref, k_ref, v_ref, bias_ref, lam_ref, g_ref, o_ref,
               vt_sc, m_sc, l_sc, acc_sc, *, t, lam_init):
    i = pl.program_id(2)

    @pl.when(i == 0)
    def _():
        for c in range(vt_sc.shape[0]):
            vt_sc[c] = v_ref[0, c * t:(c + 1) * t, :].astype(F32).T.astype(BF16)

    q = q_ref[0]
    lane = lax.broadcasted_iota(I32, q.shape, 1)
    zero = jnp.zeros_like(q)
    q2 = jnp.concatenate([jnp.where(lane < DA_HEAD_DIM, q, zero),
                          jnp.where(lane >= DA_HEAD_DIM, q, zero)], axis=0)

    m_sc[...] = jnp.full(m_sc.shape, NEG_BIG, F32)
    l_sc[...] = jnp.zeros(l_sc.shape, F32)
    acc_sc[...] = jnp.zeros(acc_sc.shape, F32)

    def block(j, bias):
        start = pl.multiple_of(j * t, t)
        k = k_ref[0, pl.ds(start, t), :]
        st = lax.dot_general(k, q2, (((1,), (1,)), ((), ())),
                             preferred_element_type=F32)
        if bias is not None:
            st = st + jnp.concatenate([bias, bias], axis=1)
        m_prev = m_sc[...]
        m_new = jnp.maximum(m_prev, jnp.max(st, axis=0, keepdims=True))
        alpha = jnp.exp2(m_prev - m_new)
        p = jnp.exp2(st - m_new)
        l_sc[...] = alpha * l_sc[...] + jnp.sum(p, axis=0, keepdims=True)
        acc_sc[...] = alpha * acc_sc[...] + jnp.dot(vt_sc[j], p.astype(BF16),
                                                    preferred_element_type=F32)
        m_sc[...] = m_new

    def far_block(j, carry):
        block(j, None)
        return carry

    lax.fori_loop(0, jnp.maximum(i - 1, 0), far_block, 0)

    @pl.when(i >= 1)
    def _():
        block(i - 1, bias_ref[0, 1])

    block(i, bias_ref[0, 0])

    lv = lam_ref[...]
    dot_a = jnp.sum(lv[0:1] * lv[1:2], axis=-1, keepdims=True)
    dot_b = jnp.sum(lv[2:3] * lv[3:4], axis=-1, keepdims=True)
    lam = jnp.exp(dot_a) - jnp.exp(dot_b) + lam_init

    o_all = acc_sc[...] / l_sc[...]
    o = o_all[:, :t] - lam * o_all[:, t:]
    ms = jnp.mean(o * o, axis=0, keepdims=True)
    o = o * lax.rsqrt(ms + LN_EPS) * g_ref[...] * (1.0 - lam_init)
    o_ref[0] = o.T.astype(o_ref.dtype)


def _diff_attention(qkv, bias_tiles, lam_vecs, subln_g, lam_init, t):
    b, s, d3 = qkv.shape
    d = d3 // 3
    heads = DA_HEADS
    assert d == heads * DA_VDIM and DA_VDIM == LANES and s % t == 0
    return pl.pallas_call(
        functools.partial(_attn_body, t=t, lam_init=lam_init),
        out_shape=jax.ShapeDtypeStruct((b, s, d), BF16),
        grid=(b, heads, s // t),
        in_specs=[
            pl.BlockSpec((1, t, LANES), lambda bi, h, i: (bi, i, h)),
            pl.BlockSpec((1, s, LANES), lambda bi, h, i: (bi, 0, heads + h)),
            pl.BlockSpec((1, s, LANES), lambda bi, h, i: (bi, 0, 2 * heads + h)),
            pl.BlockSpec((1, 2, t, t), lambda bi, h, i: (h, 0, 0, 0)),
            pl.BlockSpec((4, DA_HEAD_DIM), lambda bi, h, i: (0, 0)),
            pl.BlockSpec((DA_VDIM, 1), lambda bi, h, i: (0, 0)),
        ],
        out_specs=pl.BlockSpec((1, t, LANES), lambda bi, h, i: (bi, i, h)),
        scratch_shapes=[pltpu.VMEM((s // t, DA_VDIM, t), BF16),
                        pltpu.VMEM((1, 2 * t), F32),
                        pltpu.VMEM((1, 2 * t), F32),
                        pltpu.VMEM((DA_VDIM, 2 * t), F32)],
        compiler_params=_params(("parallel", "parallel", "arbitrary")),
        name="diff_attention",
    )(qkv, qkv, qkv, bias_tiles, lam_vecs, subln_g.reshape(DA_VDIM, 1))


def _route(logits):
    lane = lax.broadcasted_iota(I32, logits.shape, 1)
    lane_f = lane.astype(F32)
    far = float(2 * LANES)
    is_g = lane < MOE_GROUPS
    gl = jnp.where(is_g, logits, NEG_BIG)
    gmax = jnp.max(gl, axis=-1, keepdims=True)
    gidx = jnp.min(jnp.where(gl == gmax, lane_f, far), axis=-1, keepdims=True)
    gsum = jnp.sum(jnp.where(is_g, jnp.exp(gl - gmax), 0.0), axis=-1, keepdims=True)
    g_gate = 1.0 / gsum
    lo = MOE_GROUPS + MOE_PER_GROUP * gidx
    in_grp = (lane_f >= lo) & (lane_f < lo + MOE_PER_GROUP)
    el = jnp.where(in_grp, logits, NEG_BIG)
    v1 = jnp.max(el, axis=-1, keepdims=True)
    i1 = jnp.min(jnp.where(el == v1, lane_f, far), axis=-1, keepdims=True)
    el2 = jnp.where(lane_f == i1, NEG_BIG, el)
    v2 = jnp.max(el2, axis=-1, keepdims=True)
    i2 = jnp.min(jnp.where(el2 == v2, lane_f, far), axis=-1, keepdims=True)
    e21 = jnp.exp(v2 - v1)
    w1 = g_gate / (1.0 + e21)
    w2 = g_gate * e21 / (1.0 + e21)
    out = jnp.where(lane == 0, i1 - MOE_GROUPS, 0.0)
    out = jnp.where(lane == 1, i2 - MOE_GROUPS, out)
    out = jnp.where(lane == 2, w1, out)
    out = jnp.where(lane == 3, w2, out)
    return out


def _proj_ln_body(a_ref, w_ref, h_ref, g_ref, b_ref, wr_ref, br_ref, hout_ref, route_ref):
    mix = jnp.dot(a_ref[...], w_ref[...], preferred_element_type=F32)
    hn = _layer_norm(DN_ALPHA * h_ref[...] + mix, g_ref[...], b_ref[...])
    hout_ref[...] = hn
    logits = jnp.dot(hn, wr_ref[...], precision=lax.Precision.HIGHEST,
                     preferred_element_type=F32) + br_ref[...]
    route_ref[...] = _route(logits)


def _proj_ln_route(a, w_bf16, h, ln_g, ln_b, w_route, b_route):
    n, k = a.shape
    d = w_bf16.shape[1]
    tm = min(ROW_TILE, n)
    const = lambda i: (0, 0)
    return pl.pallas_call(
        _proj_ln_body,
        out_shape=(jax.ShapeDtypeStruct((n, d), F32),
                   jax.ShapeDtypeStruct((n, LANES), F32)),
        grid=(n // tm,),
        in_specs=[pl.BlockSpec((tm, k), lambda i: (i, 0)),
                  pl.BlockSpec((k, d), const),
                  pl.BlockSpec((tm, d), lambda i: (i, 0)),
                  pl.BlockSpec((1, d), const),
                  pl.BlockSpec((1, d), const),
                  pl.BlockSpec((d, LANES), const),
                  pl.BlockSpec((1, LANES), const)],
        out_specs=(pl.BlockSpec((tm, d), lambda i: (i, 0)),
                   pl.BlockSpec((tm, LANES), lambda i: (i, 0))),
        compiler_params=_params(("parallel",)),
        name="proj_ln_route",
    )(a, w_bf16, h, ln_g.reshape(1, d), ln_b.reshape(1, d), w_route, b_route)


def _router_params(w_group, b_group, w_router, b_router):
    d = w_group.shape[0]
    pad = LANES - MOE_GROUPS - MOE_EXPERTS
    w = jnp.concatenate([w_group, w_router, jnp.zeros((d, pad), F32)], axis=1)
    b = jnp.concatenate([b_group, b_router, jnp.zeros((pad,), F32)]).reshape(1, LANES)
    return w, b


def _moe_meta(e_ids, tm):
    n = e_ids.shape[0]
    a = 2 * n
    a_exp = e_ids.reshape(a)
    onehot = (a_exp[:, None] == jnp.arange(MOE_EXPERTS, dtype=I32)[None, :]).astype(I32)
    csum = jnp.cumsum(onehot, axis=0)
    rank = jnp.take_along_axis(csum, a_exp[:, None], axis=1)[:, 0] - 1
    counts = csum[-1]
    padded = (counts + tm - 1) // tm * tm
    pends = jnp.cumsum(padded)
    pstarts = pends - padded
    dest = pstarts[a_exp] + rank
    nb = a // tm + MOE_EXPERTS
    row_tok = jnp.zeros((nb * tm,), I32).at[dest].set(jnp.arange(a, dtype=I32) // 2)
    nvalid = (pends[-1] // tm).astype(I32)
    bstart = jnp.arange(nb, dtype=I32) * tm
    block_exp = jnp.minimum(jnp.searchsorted(pends, bstart, side="right"),
                            MOE_EXPERTS - 1).astype(I32)
    block_exp = jnp.where(jnp.arange(nb) < nvalid, block_exp, block_exp[nvalid - 1])
    return row_tok.reshape(nb, 1, tm), block_exp, nvalid.reshape(1), dest.reshape(n, 2)


def _row_gather_start(idx_ref, n_rows, src_hbm, dst_buf, slot, sem):
    def body(r, carry):
        row = idx_ref[0, 0, r]
        pltpu.make_async_copy(src_hbm.at[pl.ds(row, 1), :],
                              dst_buf.at[slot, pl.ds(r, 1), :],
                              sem.at[slot]).start()
        return carry
    lax.fori_loop(0, n_rows, body, 0, unroll=8)


def _row_gather_wait(n_rows, src_hbm, dst_buf, slot, sem):
    pltpu.make_async_copy(src_hbm.at[pl.ds(0, n_rows), :], dst_buf.at[slot],
                          sem.at[slot]).wait()


def _gmm_body(be_ref, nv_ref, idx_cur_ref, idx_nxt_ref, x_hbm, wg_ref, wu_ref, wd_ref,
              y_ref, xbuf, sem, *, tm):
    i = pl.program_id(0)
    nv = nv_ref[0]

    @pl.when(i == 0)
    def _():
        _row_gather_start(idx_cur_ref, tm, x_hbm, xbuf, 0, sem)

    @pl.when(i + 1 < nv)
    def _():
        _row_gather_start(idx_nxt_ref, tm, x_hbm, xbuf, (i + 1) % 2, sem)

    @pl.when(i < nv)
    def _():
        slot = i % 2
        _row_gather_wait(tm, x_hbm, xbuf, slot, sem)
        x = xbuf[slot].astype(BF16)
        gate = jnp.dot(x, wg_ref[0], preferred_element_type=F32)
        up = jnp.dot(x, wu_ref[0], preferred_element_type=F32)
        hid = (gate / (1.0 + jnp.exp(-gate)) * up).astype(BF16)
        y_ref[...] = jnp.dot(hid, wd_ref[0], preferred_element_type=F32)

    @pl.when(i >= nv)
    def _():
        y_ref[...] = jnp.zeros(y_ref.shape, y_ref.dtype)


def _grouped_experts(x, row_tok, block_exp, nvalid, wg, wu, wd, tm):
    n, d = x.shape
    nb = row_tok.shape[0]
    hd = wg.shape[2]
    clamp = lambda i, nv: jnp.minimum(i, nv[0] - 1)
    grid_spec = pltpu.PrefetchScalarGridSpec(
        num_scalar_prefetch=2,
        grid=(nb,),
        in_specs=[
            pl.BlockSpec((1, 1, tm), lambda i, be, nv: (clamp(i, nv), 0, 0),
                         memory_space=pltpu.SMEM),
            pl.BlockSpec((1, 1, tm), lambda i, be, nv: (clamp(i + 1, nv), 0, 0),
                         memory_space=pltpu.SMEM),
            pl.BlockSpec(memory_space=pl.ANY),
            pl.BlockSpec((1, d, hd), lambda i, be, nv: (be[i], 0, 0)),
            pl.BlockSpec((1, d, hd), lambda i, be, nv: (be[i], 0, 0)),
            pl.BlockSpec((1, hd, d), lambda i, be, nv: (be[i], 0, 0)),
        ],
        out_specs=pl.BlockSpec((tm, d), lambda i, be, nv: (i, 0)),
        scratch_shapes=[pltpu.VMEM((2, tm, d), x.dtype),
                        pltpu.SemaphoreType.DMA((2,))],
    )
    return pl.pallas_call(
        functools.partial(_gmm_body, tm=tm),
        out_shape=jax.ShapeDtypeStruct((nb * tm, d), F32),
        grid_spec=grid_spec,
        compiler_params=_params(("arbitrary",)),
        name="grouped_experts",
    )(block_exp, nvalid, row_tok, row_tok, x, wg, wu, wd)


def _combine_body(pos_cur_ref, pos_nxt_ref, y_hbm, h_ref, route_ref, g_ref, b_ref,
                  *rest, tm, n_tiles, with_bf16):
    if with_bf16:
        out_ref, out_bf_ref, ybuf, sem = rest
    else:
        out_ref, ybuf, sem = rest
    i = pl.program_id(0)

    @pl.when(i == 0)
    def _():
        _row_gather_start(pos_cur_ref, 2 * tm, y_hbm, ybuf, 0, sem)

    @pl.when(i + 1 < n_tiles)
    def _():
        _row_gather_start(pos_nxt_ref, 2 * tm, y_hbm, ybuf, (i + 1) % 2, sem)

    slot = i % 2
    _row_gather_wait(2 * tm, y_hbm, ybuf, slot, sem)
    y1 = ybuf[slot, 0:tm, :]
    y2 = ybuf[slot, tm:2 * tm, :]
    route = route_ref[...]
    ffn = route[:, 2:3] * y1 + route[:, 3:4] * y2
    hn = _layer_norm(DN_ALPHA * h_ref[...] + ffn, g_ref[...], b_ref[...])
    out_ref[...] = hn
    if with_bf16:
        out_bf_ref[...] = hn.astype(BF16)


def _combine_ln(y_sorted, pos, h, route, ln_g, ln_b, with_bf16):
    n, d = h.shape
    tm = min(COMBINE_TILE, n)
    n_tiles = n // tm
    pos_tiles = pos.reshape(n_tiles, tm, 2).transpose(0, 2, 1).reshape(n_tiles, 1, 2 * tm)
    const = lambda i: (0, 0)
    out_shape = [jax.ShapeDtypeStruct((n, d), F32)]
    out_specs = [pl.BlockSpec((tm, d), lambda i: (i, 0))]
    if with_bf16:
        out_shape.append(jax.ShapeDtypeStruct((n, d), BF16))
        out_specs.append(pl.BlockSpec((tm, d), lambda i: (i, 0)))
    return pl.pallas_call(
        functools.partial(_combine_body, tm=tm, n_tiles=n_tiles, with_bf16=with_bf16),
        out_shape=tuple(out_shape),
        grid=(n_tiles,),
        in_specs=[
            pl.BlockSpec((1, 1, 2 * tm), lambda i: (i, 0, 0), memory_space=pltpu.SMEM),
            pl.BlockSpec((1, 1, 2 * tm), lambda i: (jnp.minimum(i + 1, n_tiles - 1), 0, 0),
                         memory_space=pltpu.SMEM),
            pl.BlockSpec(memory_space=pl.ANY),
            pl.BlockSpec((tm, d), lambda i: (i, 0)),
            pl.BlockSpec((tm, LANES), lambda i: (i, 0)),
            pl.BlockSpec((1, d), const),
            pl.BlockSpec((1, d), const),
        ],
        out_specs=tuple(out_specs),
        scratch_shapes=[pltpu.VMEM((2, 2 * tm, d), y_sorted.dtype),
                        pltpu.SemaphoreType.DMA((2,))],
        compiler_params=_params(("arbitrary",)),
        name="moe_combine_ln",
    )(pos_tiles, pos_tiles, y_sorted, h, route, ln_g.reshape(1, d), ln_b.reshape(1, d))


def _hier_moe_ln(h, route, w_gate, w_up, w_down, ln_g, ln_b, with_bf16):
    e_ids = route[:, 0:2].astype(I32)
    row_tok, block_exp, nvalid, pos = _moe_meta(e_ids, MOE_TILE)
    y_sorted = _grouped_experts(h, row_tok, block_exp, nvalid,
                                w_gate.astype(BF16), w_up.astype(BF16),
                                w_down.astype(BF16), MOE_TILE)
    return _combine_ln(y_sorted, pos, h, route, ln_g, ln_b, with_bf16)


def _gelu_tanh(x):
    c = math.sqrt(2.0 / math.pi)
    return 0.5 * x * (1.0 + jnp.tanh(c * (x + 0.044715 * (x * x * x))))


def _sg_body(x_ref, w_ref, b_ref, lng_ref, lnb_ref, ws_ref, bs_ref, o_ref, *, half, gdim):
    x = x_ref[...]
    v = _gelu_tanh(jnp.dot(x, w_ref[:, half:], preferred_element_type=F32) + b_ref[:, half:])
    vn = _layer_norm(v, lng_ref[...], lnb_ref[...]).astype(BF16)
    for g in range(SG_GROUPS):
        cols = slice(g * gdim, (g + 1) * gdim)
        u = _gelu_tanh(jnp.dot(x, w_ref[:, cols], preferred_element_type=F32) + b_ref[:, cols])
        s = jnp.dot(ws_ref[g], vn[:, cols], preferred_element_type=F32) + bs_ref[:, g:g + 1]
        o_ref[:, cols] = (u * s).astype(o_ref.dtype)


def _spatial_gating(x_bf16, w_in, b_in, ln_g, ln_b, w_s, b_s):
    n, d = x_bf16.shape
    half = w_in.shape[1] // 2
    gdim = half // SG_GROUPS
    tm = SG_TILE
    reps = tm // SG_CHUNK
    causal = np.tril(np.ones((SG_CHUNK, SG_CHUNK), dtype=bool))
    w_tri = jnp.where(causal[None], w_s, 0.0).astype(BF16)
    eye = jnp.eye(reps, dtype=BF16)
    w_blk = jnp.einsum("ab,gts->gatbs", eye, w_tri).reshape(SG_GROUPS, tm, tm)
    bs_tile = jnp.tile(jnp.transpose(b_s), (reps, 1))
    const2 = lambda i: (0, 0)
    return pl.pallas_call(
        functools.partial(_sg_body, half=half, gdim=gdim),
        out_shape=jax.ShapeDtypeStruct((n, half), BF16),
        grid=(n // tm,),
        in_specs=[pl.BlockSpec((tm, d), lambda i: (i, 0)),
                  pl.BlockSpec((d, 2 * half), const2),
                  pl.BlockSpec((1, 2 * half), const2),
                  pl.BlockSpec((1, half), const2),
                  pl.BlockSpec((1, half), const2),
                  pl.BlockSpec((SG_GROUPS, tm, tm), lambda i: (0, 0, 0)),
                  pl.BlockSpec((tm, SG_GROUPS), const2)],
        out_specs=pl.BlockSpec((tm, half), lambda i: (i, 0)),
        compiler_params=_params(("parallel",)),
        name="spatial_gating",
    )(x_bf16, w_in.astype(BF16), b_in.reshape(1, -1), ln_g.reshape(1, -1),
      ln_b.reshape(1, -1), w_blk, bs_tile)


def kernel(x, rel_bias, attn_w_in, attn_w_out, attn_lambda, attn_subln_g, sg_w_in, sg_b_in,
           sg_ln_g, sg_ln_b, sg_w_s, sg_b_s, sg_w_out, moe_w_group, moe_b_group, moe_w_router,
           moe_b_router, moe_w_gate, moe_w_up, moe_w_down, ln_g, ln_b):
    b, s, d = x.shape
    n = b * s
    h = x.reshape(n, d)

    lam_init = 0.8 - 0.6 * math.exp(-0.3 * 0)
    t = min(ATTN_TILE, s)
    qkv = _qkv_proj(h, attn_w_in[0].astype(BF16)).reshape(b, s, 3 * d)
    bias_tiles = _bias_tiles(rel_bias, t, s)
    o = _diff_attention(qkv, bias_tiles, attn_lambda[0], attn_subln_g[0], lam_init, t)
    wr, br = _router_params(moe_w_group[0], moe_b_group[0], moe_w_router[0], moe_b_router[0])
    h, route = _proj_ln_route(o.reshape(n, d), attn_w_out[0].astype(BF16), h,
                              ln_g[0, 0], ln_b[0, 0], wr, br)
    h, h_bf16 = _hier_moe_ln(h, route, moe_w_gate[0], moe_w_up[0], moe_w_down[0],
                             ln_g[0, 1], ln_b[0, 1], with_bf16=True)

    gated = _spatial_gating(h_bf16, sg_w_in[0], sg_b_in[0], sg_ln_g[0], sg_ln_b[0],
                            sg_w_s[0], sg_b_s[0])
    wr, br = _router_params(moe_w_group[1], moe_b_group[1], moe_w_router[1], moe_b_router[1])
    h, route = _proj_ln_route(gated, sg_w_out[0].astype(BF16), h,
                              ln_g[1, 0], ln_b[1, 0], wr, br)
    (h,) = _hier_moe_ln(h, route, moe_w_gate[1], moe_w_up[1], moe_w_down[1],
                        ln_g[1, 1], ln_b[1, 1], with_bf16=False)
    return h.reshape(b, s, d)
```

```python
import functools
import math

import numpy as np
import jax
import jax.numpy as jnp
from jax import lax
from jax.experimental import pallas as pl
from jax.experimental.pallas import tpu as pltpu

F32 = jnp.float32
BF16 = jnp.bfloat16
I32 = jnp.int32

DA_HEADS = 8
DA_HEAD_DIM = 64
DA_VDIM = 2 * DA_HEAD_DIM
REL_BUCKETS = 32
REL_MAX_DIST = 128
SG_CHUNK = 128
SG_GROUPS = 8
MOE_GROUPS = 8
MOE_PER_GROUP = 8
MOE_EXPERTS = MOE_GROUPS * MOE_PER_GROUP
LN_EPS = 1e-5
DEPTH = 2
DN_ALPHA = (2 * DEPTH) ** 0.25

LANES = 128
NEG_BIG = -1e30
LOG2E = math.log2(math.e)
VMEM_LIMIT = 52 * 1024 * 1024

ROW_TILE = 512
ATTN_TILE = 512
SG_TILE = 2 * SG_CHUNK
MOE_TILE = 256
TOKEN_TILE = 256
META_ROWS = 8


def _params(sem):
    return pltpu.CompilerParams(dimension_semantics=sem, vmem_limit_bytes=VMEM_LIMIT)


def _layer_norm(y, g, b):
    mu = jnp.mean(y, axis=-1, keepdims=True)
    yc = y - mu
    var = jnp.mean(yc * yc, axis=-1, keepdims=True)
    return yc * lax.rsqrt(var + LN_EPS) * g + b


def _qkv_body(x_ref, w_ref, o_ref, *, d_model, q_scale):
    x = x_ref[...].astype(BF16)
    for c in range(3):
        cols = slice(c * d_model, (c + 1) * d_model)
        acc = jnp.dot(x, w_ref[:, cols], preferred_element_type=F32)
        if c == 0:
            acc = acc * q_scale
        o_ref[:, cols] = acc.astype(BF16)


def _qkv_proj(x2d, w_bf16):
    n, d = x2d.shape
    tm = min(ROW_TILE, n)
    return pl.pallas_call(
        functools.partial(_qkv_body, d_model=d, q_scale=DA_HEAD_DIM ** -0.5 * LOG2E),
        out_shape=jax.ShapeDtypeStruct((n, 3 * d), BF16),
        grid=(n // tm,),
        in_specs=[pl.BlockSpec((tm, d), lambda i: (i, 0)),
                  pl.BlockSpec((d, 3 * d), lambda i: (0, 0))],
        out_specs=pl.BlockSpec((tm, 3 * d), lambda i: (i, 0)),
        compiler_params=_params(("parallel",)),
        name="qkv_proj",
    )(x2d, w_bf16)


def _t5_bucket_np(dist):
    n = np.maximum(dist, 0)
    max_exact = REL_BUCKETS // 2
    nf = np.maximum(n, 1).astype(np.float64)
    large = max_exact + (np.log(nf / max_exact) / math.log(REL_MAX_DIST / max_exact)
                         * (REL_BUCKETS - max_exact)).astype(np.int64)
    large = np.minimum(large, REL_BUCKETS - 1)
    return np.where(n < max_exact, n, large)


def _bucket_tiles(t, seq):
    assert np.all(_t5_bucket_np(np.arange(t + 1, max(seq, t + 2))) == REL_BUCKETS - 1)
    kk = np.arange(t)[:, None]
    qq = np.arange(t)[None, :]
    diag = np.where(qq >= kk, _t5_bucket_np(qq - kk), -1)
    sub = _t5_bucket_np(t + qq - kk)
    return np.stack([diag, sub]).astype(np.int32)


def _bias_body(tbl_ref, bucket_ref, o_ref):
    h = pl.program_id(0)
    last = tbl_ref[REL_BUCKETS - 1, h]
    bk = bucket_ref[...]
    acc = jnp.zeros(bk.shape, F32)
    for b in range(REL_BUCKETS - 1):
        acc = jnp.where(bk == b, (tbl_ref[b, h] - last) * LOG2E, acc)
    o_ref[0] = jnp.where(bk < 0, NEG_BIG, acc)


def _bias_tiles(rel_bias, t, seq):
    heads = rel_bias.shape[1]
    buckets = jnp.asarray(_bucket_tiles(t, seq))
    return pl.pallas_call(
        _bias_body,
        out_shape=jax.ShapeDtypeStruct((heads, 2, t, t), F32),
        grid=(heads,),
        in_specs=[pl.BlockSpec(memory_space=pltpu.SMEM),
                  pl.BlockSpec((2, t, t), lambda h: (0, 0, 0))],
        out_specs=pl.BlockSpec((1, 2, t, t), lambda h: (h, 0, 0, 0)),
        compiler_params=_params(("arbitrary",)),
        name="rel_bias_tiles",
    )(rel_bias, buckets)


def _attn_body(q_ref, k_ref, v_ref, bias_ref, lam_ref, g_ref, o_ref,
               vt_sc, m_sc, l_sc, acc_sc, *, t, lam_init):
    i = pl.program_id(2)

    @pl.when(i == 0)
    def _():
        for c in range(vt_sc.shape[0]):
            vt_sc[c] = v_ref[0, c * t:(c + 1) * t, :].astype(F32).T.astype(BF16)

    q = q_ref[0]
    lane = lax.broadcasted_iota(I32, q.shape, 1)
    zero = jnp.zeros_like(q)
    q2 = jnp.concatenate([jnp.where(lane < DA_HEAD_DIM, q, zero),
                          jnp.where(lane >= DA_HEAD_DIM, q, zero)], axis=0)

    m_sc[...] = jnp.full(m_sc.shape, NEG_BIG, F32)
    l_sc[...] = jnp.zeros(l_sc.shape, F32)
    acc_sc[...] = jnp.zeros(acc_sc.shape, F32)

    def block(j, bias):
        start = pl.multiple_of(j * t, t)
        k = k_ref[0, pl.ds(start, t), :]
        st = lax.dot_general(k, q2, (((1,), (1,)), ((), ())),
                             preferred_element_type=F32)
        if bias is not None:
            st = st + jnp.concatenate([bias, bias], axis=1)
        m_prev = m_sc[...]
        m_new = jnp.maximum(m_prev, jnp.max(st, axis=0, keepdims=True))
        alpha = jnp.exp2(m_prev - m_new)
        p = jnp.exp2(st - m_new)
        l_sc[...] = alpha * l_sc[...] + jnp.sum(p, axis=0, keepdims=True)
        acc_sc[...] = alpha * acc_sc[...] + jnp.dot(vt_sc[j], p.astype(BF16),
                                                    preferred_element_type=F32)
        m_sc[...] = m_new

    def far_block(j, carry):
        block(j, None)
        return carry

    lax.fori_loop(0, jnp.maximum(i - 1, 0), far_block, 0)

    @pl.when(i >= 1)
    def _():
        block(i - 1, bias_ref[0, 1])

    block(i, bias_ref[0, 0])

    lv = lam_ref[...]
    dot_a = jnp.sum(lv[0:1] * lv[1:2], axis=-1, keepdims=True)
    dot_b = jnp.sum(lv[2:3] * lv[3:4], axis=-1, keepdims=True)
    lam = jnp.exp(dot_a) - jnp.exp(dot_b) + lam_init

    o_all = acc_sc[...] / l_sc[...]
    o = o_all[:, :t] - lam * o_all[:, t:]
    ms = jnp.mean(o * o, axis=0, keepdims=True)
    o = o * lax.rsqrt(ms + LN_EPS) * g_ref[...] * (1.0 - lam_init)
    o_ref[0] = o.T.astype(o_ref.dtype)


def _diff_attention(qkv, bias_tiles, lam_vecs, subln_g, lam_init, t):
    b, s, d3 = qkv.shape
    d = d3 // 3
    heads = DA_HEADS
    assert d == heads * DA_VDIM and DA_VDIM == LANES and s % t == 0
    return pl.pallas_call(
        functools.partial(_attn_body, t=t, lam_init=lam_init),
        out_shape=jax.ShapeDtypeStruct((b, s, d), BF16),
        grid=(b, heads, s // t),
        in_specs=[
            pl.BlockSpec((1, t, LANES), lambda bi, h, i: (bi, i, h)),
            pl.BlockSpec((1, s, LANES), lambda bi, h, i: (bi, 0, heads + h)),
            pl.BlockSpec((1, s, LANES), lambda bi, h, i: (bi, 0, 2 * heads + h)),
            pl.BlockSpec((1, 2, t, t), lambda bi, h, i: (h, 0, 0, 0)),
            pl.BlockSpec((4, DA_HEAD_DIM), lambda bi, h, i: (0, 0)),
            pl.BlockSpec((DA_VDIM, 1), lambda bi, h, i: (0, 0)),
        ],
        out_specs=pl.BlockSpec((1, t, LANES), lambda bi, h, i: (bi, i, h)),
        scratch_shapes=[pltpu.VMEM((s // t, DA_VDIM, t), BF16),
                        pltpu.VMEM((1, 2 * t), F32),
                        pltpu.VMEM((1, 2 * t), F32),
                        pltpu.VMEM((DA_VDIM, 2 * t), F32)],
        compiler_params=_params(("parallel", "parallel", "arbitrary")),
        name="diff_attention",
    )(qkv, qkv, qkv, bias_tiles, lam_vecs, subln_g.reshape(DA_VDIM, 1))


def _route(logits):
    lane = lax.broadcasted_iota(I32, logits.shape, 1)
    lane_f = lane.astype(F32)
    far = float(2 * LANES)
    is_g = lane < MOE_GROUPS
    gl = jnp.where(is_g, logits, NEG_BIG)
    gmax = jnp.max(gl, axis=-1, keepdims=True)
    gidx = jnp.min(jnp.where(gl == gmax, lane_f, far), axis=-1, keepdims=True)
    gsum = jnp.sum(jnp.where(is_g, jnp.exp(gl - gmax), 0.0), axis=-1, keepdims=True)
    g_gate = 1.0 / gsum
    lo = MOE_GROUPS + MOE_PER_GROUP * gidx
    in_grp = (lane_f >= lo) & (lane_f < lo + MOE_PER_GROUP)
    el = jnp.where(in_grp, logits, NEG_BIG)
    v1 = jnp.max(el, axis=-1, keepdims=True)
    i1 = jnp.min(jnp.where(el == v1, lane_f, far), axis=-1, keepdims=True)
    el2 = jnp.where(lane_f == i1, NEG_BIG, el)
    v2 = jnp.max(el2, axis=-1, keepdims=True)
    i2 = jnp.min(jnp.where(el2 == v2, lane_f, far), axis=-1, keepdims=True)
    e21 = jnp.exp(v2 - v1)
    w1 = g_gate / (1.0 + e21)
    w2 = g_gate * e21 / (1.0 + e21)
    out = jnp.where(lane == 0, i1 - MOE_GROUPS, 0.0)
    out = jnp.where(lane == 1, i2 - MOE_GROUPS, out)
    out = jnp.where(lane == 2, w1, out)
    out = jnp.where(lane == 3, w2, out)
    return out


def _proj_ln_body(a_ref, w_ref, h_ref, g_ref, b_ref, wr_ref, br_ref, hout_ref, route_ref):
    mix = jnp.dot(a_ref[...], w_ref[...], preferred_element_type=F32)
    hn = _layer_norm(DN_ALPHA * h_ref[...] + mix, g_ref[...], b_ref[...])
    hout_ref[...] = hn
    x_hi = hn.astype(BF16)
    x_lo = (hn - x_hi.astype(F32)).astype(BF16)
    hi_part = jnp.dot(x_hi, wr_ref[...], preferred_element_type=F32)
    lo_part = jnp.dot(x_lo, wr_ref[:, :LANES], preferred_element_type=F32)
    logits = hi_part[:, :LANES] + (hi_part[:, LANES:] + lo_part) + br_ref[...]
    route_ref[...] = _route(logits)


def _proj_ln_route(a, w_bf16, h, ln_g, ln_b, w_route, b_route):
    n, k = a.shape
    d = w_bf16.shape[1]
    tm = min(ROW_TILE, n)
    const = lambda i: (0, 0)
    return pl.pallas_call(
        _proj_ln_body,
        out_shape=(jax.ShapeDtypeStruct((n, d), F32),
                   jax.ShapeDtypeStruct((n, LANES), F32)),
        grid=(n // tm,),
        in_specs=[pl.BlockSpec((tm, k), lambda i: (i, 0)),
                  pl.BlockSpec((k, d), const),
                  pl.BlockSpec((tm, d), lambda i: (i, 0)),
                  pl.BlockSpec((1, d), const),
                  pl.BlockSpec((1, d), const),
                  pl.BlockSpec((d, 2 * LANES), const),
                  pl.BlockSpec((1, LANES), const)],
        out_specs=(pl.BlockSpec((tm, d), lambda i: (i, 0)),
                   pl.BlockSpec((tm, LANES), lambda i: (i, 0))),
        compiler_params=_params(("parallel",)),
        name="proj_ln_route",
    )(a, w_bf16, h, ln_g.reshape(1, d), ln_b.reshape(1, d), w_route, b_route)


def _router_params(w_group, b_group, w_router, b_router):
    d = w_group.shape[0]
    pad = LANES - MOE_GROUPS - MOE_EXPERTS
    w = jnp.concatenate([w_group, w_router, jnp.zeros((d, pad), F32)], axis=1)
    b = jnp.concatenate([b_group, b_router, jnp.zeros((pad,), F32)]).reshape(1, LANES)
    w_hi = w.astype(BF16)
    w_lo = (w - w_hi.astype(F32)).astype(BF16)
    return jnp.concatenate([w_hi, w_lo], axis=1), b


def _plan_body(route_ref, dest_ref, meta_ref, cnt_sc, carry_sc, *, tm, moe_tile):
    phase = pl.program_id(0)
    i = pl.program_id(1)
    route = route_ref[...]
    lane_i = lax.broadcasted_iota(I32, route.shape, 1)
    lane = lane_i.astype(F32)
    f1 = jnp.where(lane == route[:, 0:1], 1.0, 0.0)
    f2 = jnp.where(lane == route[:, 1:2], 1.0, 0.0)
    tot1 = jnp.sum(f1, axis=0, keepdims=True)
    tot2 = jnp.sum(f2, axis=0, keepdims=True)

    @pl.when((phase == 0) & (i == 0))
    def _():
        cnt_sc[...] = jnp.zeros(cnt_sc.shape, F32)

    @pl.when(phase == 0)
    def _():
        cnt_sc[...] += tot1 + tot2

    @pl.when((phase == 1) & (i == 0))
    def _():
        cnt = cnt_sc[...]
        padded = jnp.floor((cnt + (moe_tile - 1)) * (1.0 / moe_tile)) * moe_tile
        r = lax.broadcasted_iota(I32, (LANES, LANES), 0)
        c = lax.broadcasted_iota(I32, (LANES, LANES), 1)
        upper = jnp.where(r < c, 1.0, 0.0)
        pstart = jnp.dot(jnp.broadcast_to(padded, (META_ROWS, LANES)), upper,
                         precision=lax.Precision.HIGHEST, preferred_element_type=F32)[0:1]
        carry_sc[...] = pstart
        row = lax.broadcasted_iota(I32, (META_ROWS, LANES), 0)
        meta = jnp.where(row == 0, cnt, jnp.where(row == 1, padded, 0.0))
        meta_ref[...] = meta.astype(I32)

    @pl.when(phase == 1)
    def _():
        carry = carry_sc[...]
        rr = lax.broadcasted_iota(I32, (tm, tm), 0)
        cc = lax.broadcasted_iota(I32, (tm, tm), 1)
        lower = jnp.where(rr > cc, 1.0, 0.0).astype(BF16)
        pre1 = jnp.dot(lower, f1.astype(BF16), preferred_element_type=F32)
        pre2 = jnp.dot(lower, f2.astype(BF16), preferred_element_type=F32)
        d1 = jnp.sum(f1 * (carry + pre1), axis=1, keepdims=True)
        d2 = jnp.sum(f2 * (carry + tot1 + pre2), axis=1, keepdims=True)
        carry_sc[...] = carry + tot1 + tot2
        both = jnp.where(lane_i == 0, d1, jnp.where(lane_i == 1, d2, 0.0))
        dest_ref[0] = both.T[0:META_ROWS].astype(I32)


def _moe_plan(route, moe_tile):
    n = route.shape[0]
    tm = min(TOKEN_TILE, n)
    n_tiles = n // tm
    return pl.pallas_call(
        functools.partial(_plan_body, tm=tm, moe_tile=moe_tile),
        out_shape=(jax.ShapeDtypeStruct((n_tiles, META_ROWS, tm), I32),
                   jax.ShapeDtypeStruct((META_ROWS, LANES), I32)),
        grid=(2, n_tiles),
        in_specs=[pl.BlockSpec((tm, LANES), lambda p, i: (i, 0))],
        out_specs=(pl.BlockSpec((1, META_ROWS, tm), lambda p, i: (i * p, 0, 0)),
                   pl.BlockSpec((META_ROWS, LANES), lambda p, i: (0, 0))),
        scratch_shapes=[pltpu.VMEM((1, LANES), F32), pltpu.VMEM((1, LANES), F32)],
        compiler_params=_params(("arbitrary", "arbitrary")),
        name="moe_plan",
    )(route)


def _rows_wait(src_hbm, dst_hbm, n_rows, sem):
    pltpu.make_async_copy(src_hbm.at[pl.ds(0, n_rows), :], dst_hbm.at[pl.ds(0, n_rows), :],
                          sem).wait()


def _dispatch_body(dest_ref, h_hbm, xs_in_hbm, xs_hbm, sem, *, tm, n_tiles):
    del xs_in_hbm
    i = pl.program_id(0)
    slot = i % 2
    base = i * tm
    for r in range(tm):
        src = h_hbm.at[pl.ds(base + r, 1), :]
        for k in range(2):
            pltpu.make_async_copy(src, xs_hbm.at[pl.ds(dest_ref[0, k, r], 1), :],
                                  sem.at[slot]).start()

    @pl.when(i >= 1)
    def _():
        _rows_wait(h_hbm, xs_hbm, 2 * tm, sem.at[1 - slot])

    @pl.when(i == n_tiles - 1)
    def _():
        _rows_wait(h_hbm, xs_hbm, 2 * tm, sem.at[slot])


def _moe_dispatch(h, dest, n_rows):
    n, d = h.shape
    n_tiles, _, tm = dest.shape
    zeros = jnp.zeros((n_rows, d), h.dtype)
    return pl.pallas_call(
        functools.partial(_dispatch_body, tm=tm, n_tiles=n_tiles),
        out_shape=jax.ShapeDtypeStruct((n_rows, d), h.dtype),
        grid=(n_tiles,),
        in_specs=[pl.BlockSpec((1, META_ROWS, tm), lambda i: (i, 0, 0), memory_space=pltpu.SMEM),
                  pl.BlockSpec(memory_space=pl.ANY),
                  pl.BlockSpec(memory_space=pl.ANY)],
        out_specs=pl.BlockSpec(memory_space=pl.ANY),
        scratch_shapes=[pltpu.SemaphoreType.DMA((2,))],
        input_output_aliases={2: 0},
        compiler_params=_params(("arbitrary",)),
        name="moe_dispatch",
    )(dest, h, zeros)


def _gmm_body(be_ref, nv_ref, x_ref, wg_ref, wu_ref, wd_ref, y_ref, wg_sc, wu_sc, wd_sc):
    i = pl.program_id(0)
    nv = nv_ref[0]
    new_expert = (i == 0) | (be_ref[i] != be_ref[jnp.maximum(i - 1, 0)])

    @pl.when((i < nv) & new_expert)
    def _():
        wg_sc[...] = wg_ref[0].astype(BF16)
        wu_sc[...] = wu_ref[0].astype(BF16)
        wd_sc[...] = wd_ref[0].astype(BF16)

    @pl.when(i < nv)
    def _():
        x = x_ref[...].astype(BF16)
        gate = jnp.dot(x, wg_sc[...], preferred_element_type=F32)
        up = jnp.dot(x, wu_sc[...], preferred_element_type=F32)
        hid = (gate / (1.0 + jnp.exp(-gate)) * up).astype(BF16)
        y_ref[...] = jnp.dot(hid, wd_sc[...], preferred_element_type=F32)

    @pl.when(i >= nv)
    def _():
        y_ref[...] = jnp.zeros(y_ref.shape, y_ref.dtype)


def _grouped_experts(xs, block_exp, nvalid, wg, wu, wd, tm):
    p, d = xs.shape
    nb = p // tm
    hd = wg.shape[2]
    clamp = lambda i, nv: jnp.minimum(i, nv[0] - 1)
    grid_spec = pltpu.PrefetchScalarGridSpec(
        num_scalar_prefetch=2,
        grid=(nb,),
        in_specs=[
            pl.BlockSpec((tm, d), lambda i, be, nv: (clamp(i, nv), 0)),
            pl.BlockSpec((1, d, hd), lambda i, be, nv: (be[i], 0, 0)),
            pl.BlockSpec((1, d, hd), lambda i, be, nv: (be[i], 0, 0)),
            pl.BlockSpec((1, hd, d), lambda i, be, nv: (be[i], 0, 0)),
        ],
        out_specs=pl.BlockSpec((tm, d), lambda i, be, nv: (i, 0)),
        scratch_shapes=[pltpu.VMEM((d, hd), BF16), pltpu.VMEM((d, hd), BF16),
                        pltpu.VMEM((hd, d), BF16)],
    )
    return pl.pallas_call(
        _gmm_body,
        out_shape=jax.ShapeDtypeStruct((p, d), F32),
        grid_spec=grid_spec,
        compiler_params=_params(("arbitrary",)),
        name="grouped_experts",
    )(block_exp, nvalid, xs, wg, wu, wd)


def _combine_body(pos_cur_ref, pos_nxt_ref, y_hbm, h_ref, route_ref, g_ref, b_ref,
                  *rest, tm, n_tiles, with_bf16):
    if with_bf16:
        out_ref, out_bf_ref, ybuf, sem = rest
    else:
        out_ref, ybuf, sem = rest
    i = pl.program_id(0)
    slot = i % 2

    def start_tile(pos_ref, dst_slot):
        for r in range(tm):
            for k in range(2):
                pltpu.make_async_copy(y_hbm.at[pl.ds(pos_ref[0, k, r], 1), :],
                                      ybuf.at[dst_slot, pl.ds(k * tm + r, 1), :],
                                      sem.at[dst_slot]).start()

    def wait_tile(dst_slot):
        pltpu.make_async_copy(y_hbm.at[pl.ds(0, 2 * tm), :], ybuf.at[dst_slot],
                              sem.at[dst_slot]).wait()

    @pl.when(i == 0)
    def _():
        start_tile(pos_cur_ref, 0)

    start_tile(pos_nxt_ref, 1 - slot)
    wait_tile(slot)
    y1 = ybuf[slot, 0:tm, :]
    y2 = ybuf[slot, tm:2 * tm, :]
    route = route_ref[...]
    ffn = route[:, 2:3] * y1 + route[:, 3:4] * y2
    hn = _layer_norm(DN_ALPHA * h_ref[...] + ffn, g_ref[...], b_ref[...])
    out_ref[...] = hn
    if with_bf16:
        out_bf_ref[...] = hn.astype(BF16)

    @pl.when(i == n_tiles - 1)
    def _():
        wait_tile(1 - slot)


def _combine_ln(y_sorted, dest, h, route, ln_g, ln_b, with_bf16):
    n, d = h.shape
    n_tiles, _, tm = dest.shape
    const = lambda i: (0, 0)
    out_shape = [jax.ShapeDtypeStruct((n, d), F32)]
    out_specs = [pl.BlockSpec((tm, d), lambda i: (i, 0))]
    if with_bf16:
        out_shape.append(jax.ShapeDtypeStruct((n, d), BF16))
        out_specs.append(pl.BlockSpec((tm, d), lambda i: (i, 0)))
    return pl.pallas_call(
        functools.partial(_combine_body, tm=tm, n_tiles=n_tiles, with_bf16=with_bf16),
        out_shape=tuple(out_shape),
        grid=(n_tiles,),
        in_specs=[
            pl.BlockSpec((1, META_ROWS, tm), lambda i: (i, 0, 0), memory_space=pltpu.SMEM),
            pl.BlockSpec((1, META_ROWS, tm), lambda i: (jnp.minimum(i + 1, n_tiles - 1), 0, 0),
                         memory_space=pltpu.SMEM),
            pl.BlockSpec(memory_space=pl.ANY),
            pl.BlockSpec((tm, d), lambda i: (i, 0)),
            pl.BlockSpec((tm, LANES), lambda i: (i, 0)),
            pl.BlockSpec((1, d), const),
            pl.BlockSpec((1, d), const),
        ],
        out_specs=tuple(out_specs),
        scratch_shapes=[pltpu.VMEM((2, 2 * tm, d), y_sorted.dtype),
                        pltpu.SemaphoreType.DMA((2,))],
        compiler_params=_params(("arbitrary",)),
        name="moe_combine_ln",
    )(dest, dest, y_sorted, h, route, ln_g.reshape(1, d), ln_b.reshape(1, d))


def _hier_moe_ln(h, route, w_gate, w_up, w_down, ln_g, ln_b, with_bf16):
    n = h.shape[0]
    tile = MOE_TILE
    nb = 2 * n // tile + MOE_EXPERTS
    dest, meta = _moe_plan(route, tile)
    pends = jnp.cumsum(meta[1, :MOE_EXPERTS])
    nvalid = (pends[-1] // tile).astype(I32)
    bstart = jnp.arange(nb, dtype=I32) * tile
    block_exp = jnp.minimum(jnp.searchsorted(pends, bstart, side="right"),
                            MOE_EXPERTS - 1).astype(I32)
    block_exp = jnp.where(jnp.arange(nb) < nvalid, block_exp, block_exp[nvalid - 1])
    xs = _moe_dispatch(h, dest, nb * tile)
    y_sorted = _grouped_experts(xs, block_exp, nvalid.reshape(1), w_gate, w_up, w_down, tile)
    return _combine_ln(y_sorted, dest, h, route, ln_g, ln_b, with_bf16)


def _gelu_tanh(x):
    c = math.sqrt(2.0 / math.pi)
    return 0.5 * x * (1.0 + jnp.tanh(c * (x + 0.044715 * (x * x * x))))


def _sg_body(x_ref, w_ref, b_ref, lng_ref, lnb_ref, ws_ref, bs_ref, o_ref, *, half, gdim):
    x = x_ref[...]
    v = _gelu_tanh(jnp.dot(x, w_ref[:, half:], preferred_element_type=F32) + b_ref[:, half:])
    vn = _layer_norm(v, lng_ref[...], lnb_ref[...]).astype(BF16)
    for g in range(SG_GROUPS):
        cols = slice(g * gdim, (g + 1) * gdim)
        u = _gelu_tanh(jnp.dot(x, w_ref[:, cols], preferred_element_type=F32) + b_ref[:, cols])
        s = jnp.dot(ws_ref[g], vn[:, cols], preferred_element_type=F32) + bs_ref[:, g:g + 1]
        o_ref[:, cols] = (u * s).astype(o_ref.dtype)


def _spatial_gating(x_bf16, w_in, b_in, ln_g, ln_b, w_s, b_s):
    n, d = x_bf16.shape
    half = w_in.shape[1] // 2
    gdim = half // SG_GROUPS
    tm = SG_TILE
    reps = tm // SG_CHUNK
    causal = np.tril(np.ones((SG_CHUNK, SG_CHUNK), dtype=bool))
    w_tri = jnp.where(causal[None], w_s, 0.0).astype(BF16)
    eye = jnp.eye(reps, dtype=BF16)
    w_blk = jnp.einsum("ab,gts->gatbs", eye, w_tri).reshape(SG_GROUPS, tm, tm)
    bs_tile = jnp.tile(jnp.transpose(b_s), (reps, 1))
    const2 = lambda i: (0, 0)
    return pl.pallas_call(
        functools.partial(_sg_body, half=half, gdim=gdim),
        out_shape=jax.ShapeDtypeStruct((n, half), BF16),
        grid=(n // tm,),
        in_specs=[pl.BlockSpec((tm, d), lambda i: (i, 0)),
                  pl.BlockSpec((d, 2 * half), const2),
                  pl.BlockSpec((1, 2 * half), const2),
                  pl.BlockSpec((1, half), const2),
                  pl.BlockSpec((1, half), const2),
                  pl.BlockSpec((SG_GROUPS, tm, tm), lambda i: (0, 0, 0)),
                  pl.BlockSpec((tm, SG_GROUPS), const2)],
        out_specs=pl.BlockSpec((tm, half), lambda i: (i, 0)),
        compiler_params=_params(("parallel",)),
        name="spatial_gating",
    )(x_bf16, w_in.astype(BF16), b_in.reshape(1, -1), ln_g.reshape(1, -1),
      ln_b.reshape(1, -1), w_blk, bs_tile)


def kernel(x, rel_bias, attn_w_in, attn_w_out, attn_lambda, attn_subln_g, sg_w_in, sg_b_in,
           sg_ln_g, sg_ln_b, sg_w_s, sg_b_s, sg_w_out, moe_w_group, moe_b_group, moe_w_router,
           moe_b_router, moe_w_gate, moe_w_up, moe_w_down, ln_g, ln_b):
    b, s, d = x.shape
    n = b * s
    h = x.reshape(n, d)

    lam_init = 0.8 - 0.6 * math.exp(-0.3 * 0)
    t = min(ATTN_TILE, s)
    qkv = _qkv_proj(h, attn_w_in[0].astype(BF16)).reshape(b, s, 3 * d)
    bias_tiles = _bias_tiles(rel_bias, t, s)
    o = _diff_attention(qkv, bias_tiles, attn_lambda[0], attn_subln_g[0], lam_init, t)
    wr, br = _router_params(moe_w_group[0], moe_b_group[0], moe_w_router[0], moe_b_router[0])
    h, route = _proj_ln_route(o.reshape(n, d), attn_w_out[0].astype(BF16), h,
                              ln_g[0, 0], ln_b[0, 0], wr, br)
    h, h_bf16 = _hier_moe_ln(h, route, moe_w_gate[0], moe_w_up[0], moe_w_down[0],
                             ln_g[0, 1], ln_b[0, 1], with_bf16=True)

    gated = _spatial_gating(h_bf16, sg_w_in[0], sg_b_in[0], sg_ln_g[0], sg_ln_b[0],
                            sg_w_s[0], sg_b_s[0])
    wr, br = _router_params(moe_w_group[1], moe_b_group[1], moe_w_router[1], moe_b_router[1])
    h, route = _proj_ln_route(gated, sg_w_out[0].astype(BF16), h,
                              ln_g[1, 0], ln_b[1, 0], wr, br)
    (h,) = _hier_moe_ln(h, route, moe_w_gate[1], moe_w_up[1], moe_w_down[1],
                        ln_g[1, 1], ln_b[1, 1], with_bf16=False)
    return h.reshape(b, s, d)
```

```python
import functools
import math

import numpy as np
import jax
import jax.numpy as jnp
from jax import lax
from jax.experimental import pallas as pl
from jax.experimental.pallas import tpu as pltpu

F32 = jnp.float32
BF16 = jnp.bfloat16
I32 = jnp.int32

DA_HEADS = 8
DA_HEAD_DIM = 64
DA_VDIM = 2 * DA_HEAD_DIM
REL_BUCKETS = 32
REL_MAX_DIST = 128
SG_CHUNK = 128
SG_GROUPS = 8
MOE_GROUPS = 8
MOE_PER_GROUP = 8
MOE_EXPERTS = MOE_GROUPS * MOE_PER_GROUP
LN_EPS = 1e-5
DEPTH = 2
DN_ALPHA = (2 * DEPTH) ** 0.25

LANES = 128
NEG_BIG = -1e30
LOG2E = math.log2(math.e)
VMEM_LIMIT = 52 * 1024 * 1024

ROW_TILE = 512
ATTN_TILE = 512
SG_TILE = 2 * SG_CHUNK
MOE_TILE = 256
TOKEN_TILE = 256
META_ROWS = 8


def _params(sem):
    return pltpu.CompilerParams(dimension_semantics=sem, vmem_limit_bytes=VMEM_LIMIT)


def _layer_norm(y, g, b):
    mu = jnp.mean(y, axis=-1, keepdims=True)
    yc = y - mu
    var = jnp.mean(yc * yc, axis=-1, keepdims=True)
    return yc * lax.rsqrt(var + LN_EPS) * g + b


def _qkv_body(x_ref, w_ref, o_ref, *, d_model, q_scale):
    x = x_ref[...].astype(BF16)
    for c in range(3):
        cols = slice(c * d_model, (c + 1) * d_model)
        acc = jnp.dot(x, w_ref[:, cols], preferred_element_type=F32)
        if c == 0:
            acc = acc * q_scale
        o_ref[:, cols] = acc.astype(BF16)


def _qkv_proj(x2d, w_bf16):
    n, d = x2d.shape
    tm = min(ROW_TILE, n)
    return pl.pallas_call(
        functools.partial(_qkv_body, d_model=d, q_scale=DA_HEAD_DIM ** -0.5 * LOG2E),
        out_shape=jax.ShapeDtypeStruct((n, 3 * d), BF16),
        grid=(n // tm,),
        in_specs=[pl.BlockSpec((tm, d), lambda i: (i, 0)),
                  pl.BlockSpec((d, 3 * d), lambda i: (0, 0))],
        out_specs=pl.BlockSpec((tm, 3 * d), lambda i: (i, 0)),
        compiler_params=_params(("parallel",)),
        name="qkv_proj",
    )(x2d, w_bf16)


def _t5_bucket_np(dist):
    n = np.maximum(dist, 0)
    max_exact = REL_BUCKETS // 2
    nf = np.maximum(n, 1).astype(np.float64)
    large = max_exact + (np.log(nf / max_exact) / math.log(REL_MAX_DIST / max_exact)
                         * (REL_BUCKETS - max_exact)).astype(np.int64)
    large = np.minimum(large, REL_BUCKETS - 1)
    return np.where(n < max_exact, n, large)


def _bucket_tiles(t, seq):
    assert np.all(_t5_bucket_np(np.arange(t + 1, max(seq, t + 2))) == REL_BUCKETS - 1)
    kk = np.arange(t)[:, None]
    qq = np.arange(t)[None, :]
    diag = np.where(qq >= kk, _t5_bucket_np(qq - kk), -1)
    sub = _t5_bucket_np(t + qq - kk)
    return np.stack([diag, sub]).astype(np.int32)


def _bias_body(tbl_ref, bucket_ref, o_ref):
    h = pl.program_id(0)
    last = tbl_ref[REL_BUCKETS - 1, h]
    bk = bucket_ref[...]
    acc = jnp.zeros(bk.shape, F32)
    for b in range(REL_BUCKETS - 1):
        acc = jnp.where(bk == b, (tbl_ref[b, h] - last) * LOG2E, acc)
    o_ref[0] = jnp.where(bk < 0, NEG_BIG, acc)


def _bias_tiles(rel_bias, t, seq):
    heads = rel_bias.shape[1]
    buckets = jnp.asarray(_bucket_tiles(t, seq))
    return pl.pallas_call(
        _bias_body,
        out_shape=jax.ShapeDtypeStruct((heads, 2, t, t), F32),
        grid=(heads,),
        in_specs=[pl.BlockSpec(memory_space=pltpu.SMEM),
                  pl.BlockSpec((2, t, t), lambda h: (0, 0, 0))],
        out_specs=pl.BlockSpec((1, 2, t, t), lambda h: (h, 0, 0, 0)),
        compiler_params=_params(("arbitrary",)),
        name="rel_bias_tiles",
    )(rel_bias, buckets)


def _attn_body(q_ref, k_ref, v_ref, bias_ref, lam_ref, g_ref, o_ref,
               vt_sc, m_sc, l_sc, acc_sc, *, t, lam_init):
    i = pl.program_id(2)

    @pl.when(i == 0)
    def _():
        for c in range(vt_sc.shape[0]):
            vt_sc[c] = v_ref[0, c * t:(c + 1) * t, :].astype(F32).T.astype(BF16)

    q = q_ref[0]
    lane = lax.broadcasted_iota(I32, q.shape, 1)
    zero = jnp.zeros_like(q)
    q2 = jnp.concatenate([jnp.where(lane < DA_HEAD_DIM, q, zero),
                          jnp.where(lane >= DA_HEAD_DIM, q, zero)], axis=0)

    m_sc[...] = jnp.full(m_sc.shape, NEG_BIG, F32)
    l_sc[...] = jnp.zeros(l_sc.shape, F32)
    acc_sc[...] = jnp.zeros(acc_sc.shape, F32)

    def block(j, bias):
        start = pl.multiple_of(j * t, t)
        k = k_ref[0, pl.ds(start, t), :]
        st = lax.dot_general(k, q2, (((1,), (1,)), ((), ())),
                             preferred_element_type=F32)
        if bias is not None:
            st = st + jnp.concatenate([bias, bias], axis=1)
        m_prev = m_sc[...]
        m_new = jnp.maximum(m_prev, jnp.max(st, axis=0, keepdims=True))
        alpha = jnp.exp2(m_prev - m_new)
        p = jnp.exp2(st - m_new)
        l_sc[...] = alpha * l_sc[...] + jnp.sum(p, axis=0, keepdims=True)
        acc_sc[...] = alpha * acc_sc[...] + jnp.dot(vt_sc[j], p.astype(BF16),
                                                    preferred_element_type=F32)
        m_sc[...] = m_new

    def far_block(j, carry):
        block(j, None)
        return carry

    lax.fori_loop(0, jnp.maximum(i - 1, 0), far_block, 0)

    @pl.when(i >= 1)
    def _():
        block(i - 1, bias_ref[0, 1])

    block(i, bias_ref[0, 0])

    lv = lam_ref[...]
    dot_a = jnp.sum(lv[0:1] * lv[1:2], axis=-1, keepdims=True)
    dot_b = jnp.sum(lv[2:3] * lv[3:4], axis=-1, keepdims=True)
    lam = jnp.exp(dot_a) - jnp.exp(dot_b) + lam_init

    o_all = acc_sc[...] / l_sc[...]
    o = o_all[:, :t] - lam * o_all[:, t:]
    ms = jnp.mean(o * o, axis=0, keepdims=True)
    o = o * lax.rsqrt(ms + LN_EPS) * g_ref[...] * (1.0 - lam_init)
    o_ref[0] = o.T.astype(o_ref.dtype)


def _diff_attention(qkv, bias_tiles, lam_vecs, subln_g, lam_init, t):
    b, s, d3 = qkv.shape
    d = d3 // 3
    heads = DA_HEADS
    assert d == heads * DA_VDIM and DA_VDIM == LANES and s % t == 0
    return pl.pallas_call(
        functools.partial(_attn_body, t=t, lam_init=lam_init),
        out_shape=jax.ShapeDtypeStruct((b, s, d), BF16),
        grid=(b, heads, s // t),
        in_specs=[
            pl.BlockSpec((1, t, LANES), lambda bi, h, i: (bi, i, h)),
            pl.BlockSpec((1, s, LANES), lambda bi, h, i: (bi, 0, heads + h)),
            pl.BlockSpec((1, s, LANES), lambda bi, h, i: (bi, 0, 2 * heads + h)),
            pl.BlockSpec((1, 2, t, t), lambda bi, h, i: (h, 0, 0, 0)),
            pl.BlockSpec((4, DA_HEAD_DIM), lambda bi, h, i: (0, 0)),
            pl.BlockSpec((DA_VDIM, 1), lambda bi, h, i: (0, 0)),
        ],
        out_specs=pl.BlockSpec((1, t, LANES), lambda bi, h, i: (bi, i, h)),
        scratch_shapes=[pltpu.VMEM((s // t, DA_VDIM, t), BF16),
                        pltpu.VMEM((1, 2 * t), F32),
                        pltpu.VMEM((1, 2 * t), F32),
                        pltpu.VMEM((DA_VDIM, 2 * t), F32)],
        compiler_params=_params(("parallel", "parallel", "arbitrary")),
        name="diff_attention",
    )(qkv, qkv, qkv, bias_tiles, lam_vecs, subln_g.reshape(DA_VDIM, 1))


def _route(logits):
    lane = lax.broadcasted_iota(I32, logits.shape, 1)
    lane_f = lane.astype(F32)
    far = float(2 * LANES)
    is_g = lane < MOE_GROUPS
    gl = jnp.where(is_g, logits, NEG_BIG)
    gmax = jnp.max(gl, axis=-1, keepdims=True)
    gidx = jnp.min(jnp.where(gl == gmax, lane_f, far), axis=-1, keepdims=True)
    gsum = jnp.sum(jnp.where(is_g, jnp.exp(gl - gmax), 0.0), axis=-1, keepdims=True)
    g_gate = 1.0 / gsum
    lo = MOE_GROUPS + MOE_PER_GROUP * gidx
    in_grp = (lane_f >= lo) & (lane_f < lo + MOE_PER_GROUP)
    el = jnp.where(in_grp, logits, NEG_BIG)
    v1 = jnp.max(el, axis=-1, keepdims=True)
    i1 = jnp.min(jnp.where(el == v1, lane_f, far), axis=-1, keepdims=True)
    el2 = jnp.where(lane_f == i1, NEG_BIG, el)
    v2 = jnp.max(el2, axis=-1, keepdims=True)
    i2 = jnp.min(jnp.where(el2 == v2, lane_f, far), axis=-1, keepdims=True)
    e21 = jnp.exp(v2 - v1)
    w1 = g_gate / (1.0 + e21)
    w2 = g_gate * e21 / (1.0 + e21)
    out = jnp.where(lane == 0, i1 - MOE_GROUPS, 0.0)
    out = jnp.where(lane == 1, i2 - MOE_GROUPS, out)
    out = jnp.where(lane == 2, w1, out)
    out = jnp.where(lane == 3, w2, out)
    return out


def _proj_ln_body(a_ref, w_ref, h_ref, g_ref, b_ref, wr_ref, br_ref, hout_ref, route_ref):
    mix = jnp.dot(a_ref[...], w_ref[...], preferred_element_type=F32)
    hn = _layer_norm(DN_ALPHA * h_ref[...] + mix, g_ref[...], b_ref[...])
    hout_ref[...] = hn
    x_hi = hn.astype(BF16)
    x_lo = (hn - x_hi.astype(F32)).astype(BF16)
    hi_part = jnp.dot(x_hi, wr_ref[...], preferred_element_type=F32)
    lo_part = jnp.dot(x_lo, wr_ref[:, :LANES], preferred_element_type=F32)
    logits = hi_part[:, :LANES] + (hi_part[:, LANES:] + lo_part) + br_ref[...]
    route_ref[...] = _route(logits)


def _proj_ln_route(a, w_bf16, h, ln_g, ln_b, w_route, b_route):
    n, k = a.shape
    d = w_bf16.shape[1]
    tm = min(ROW_TILE, n)
    const = lambda i: (0, 0)
    return pl.pallas_call(
        _proj_ln_body,
        out_shape=(jax.ShapeDtypeStruct((n, d), F32),
                   jax.ShapeDtypeStruct((n, LANES), F32)),
        grid=(n // tm,),
        in_specs=[pl.BlockSpec((tm, k), lambda i: (i, 0)),
                  pl.BlockSpec((k, d), const),
                  pl.BlockSpec((tm, d), lambda i: (i, 0)),
                  pl.BlockSpec((1, d), const),
                  pl.BlockSpec((1, d), const),
                  pl.BlockSpec((d, 2 * LANES), const),
                  pl.BlockSpec((1, LANES), const)],
        out_specs=(pl.BlockSpec((tm, d), lambda i: (i, 0)),
                   pl.BlockSpec((tm, LANES), lambda i: (i, 0))),
        compiler_params=_params(("parallel",)),
        name="proj_ln_route",
    )(a, w_bf16, h, ln_g.reshape(1, d), ln_b.reshape(1, d), w_route, b_route)


def _router_params(w_group, b_group, w_router, b_router):
    d = w_group.shape[0]
    pad = LANES - MOE_GROUPS - MOE_EXPERTS
    w = jnp.concatenate([w_group, w_router, jnp.zeros((d, pad), F32)], axis=1)
    b = jnp.concatenate([b_group, b_router, jnp.zeros((pad,), F32)]).reshape(1, LANES)
    w_hi = w.astype(BF16)
    w_lo = (w - w_hi.astype(F32)).astype(BF16)
    return jnp.concatenate([w_hi, w_lo], axis=1), b


def _plan_body(route_ref, dest_ref, meta_ref, cnt_sc, carry_sc, *, tm, moe_tile):
    phase = pl.program_id(0)
    i = pl.program_id(1)
    route = route_ref[...]
    lane_i = lax.broadcasted_iota(I32, route.shape, 1)
    lane = lane_i.astype(F32)
    f1 = jnp.where(lane == route[:, 0:1], 1.0, 0.0)
    f2 = jnp.where(lane == route[:, 1:2], 1.0, 0.0)
    tot1 = jnp.sum(f1, axis=0, keepdims=True)
    tot2 = jnp.sum(f2, axis=0, keepdims=True)

    @pl.when((phase == 0) & (i == 0))
    def _():
        cnt_sc[...] = jnp.zeros(cnt_sc.shape, F32)

    @pl.when(phase == 0)
    def _():
        cnt_sc[...] += tot1 + tot2

    @pl.when((phase == 1) & (i == 0))
    def _():
        cnt = cnt_sc[...]
        padded = jnp.floor((cnt + (moe_tile - 1)) * (1.0 / moe_tile)) * moe_tile
        r = lax.broadcasted_iota(I32, (LANES, LANES), 0)
        c = lax.broadcasted_iota(I32, (LANES, LANES), 1)
        upper = jnp.where(r < c, 1.0, 0.0)
        pstart = jnp.dot(jnp.broadcast_to(padded, (META_ROWS, LANES)), upper,
                         precision=lax.Precision.HIGHEST, preferred_element_type=F32)[0:1]
        carry_sc[...] = pstart
        row = lax.broadcasted_iota(I32, (META_ROWS, LANES), 0)
        meta = jnp.where(row == 0, cnt, jnp.where(row == 1, padded, 0.0))
        meta_ref[...] = meta.astype(I32)

    @pl.when(phase == 1)
    def _():
        carry = carry_sc[...]
        rr = lax.broadcasted_iota(I32, (tm, tm), 0)
        cc = lax.broadcasted_iota(I32, (tm, tm), 1)
        lower = jnp.where(rr > cc, 1.0, 0.0).astype(BF16)
        pre1 = jnp.dot(lower, f1.astype(BF16), preferred_element_type=F32)
        pre2 = jnp.dot(lower, f2.astype(BF16), preferred_element_type=F32)
        d1 = jnp.sum(f1 * (carry + pre1), axis=1, keepdims=True)
        d2 = jnp.sum(f2 * (carry + tot1 + pre2), axis=1, keepdims=True)
        carry_sc[...] = carry + tot1 + tot2
        both = jnp.where(lane_i == 0, d1, jnp.where(lane_i == 1, d2, 0.0))
        dest_ref[0] = both.T[0:META_ROWS].astype(I32)


def _moe_plan(route, moe_tile):
    n = route.shape[0]
    tm = min(TOKEN_TILE, n)
    n_tiles = n // tm
    return pl.pallas_call(
        functools.partial(_plan_body, tm=tm, moe_tile=moe_tile),
        out_shape=(jax.ShapeDtypeStruct((n_tiles, META_ROWS, tm), I32),
                   jax.ShapeDtypeStruct((META_ROWS, LANES), I32)),
        grid=(2, n_tiles),
        in_specs=[pl.BlockSpec((tm, LANES), lambda p, i: (i, 0))],
        out_specs=(pl.BlockSpec((1, META_ROWS, tm), lambda p, i: (i * p, 0, 0)),
                   pl.BlockSpec((META_ROWS, LANES), lambda p, i: (0, 0))),
        scratch_shapes=[pltpu.VMEM((1, LANES), F32), pltpu.VMEM((1, LANES), F32)],
        compiler_params=_params(("arbitrary", "arbitrary")),
        name="moe_plan",
    )(route)


def _dispatch_body(dest_ref, h_ref, xs_in_hbm, xs_hbm, stage, sem, *, tm, n_tiles):
    del xs_in_hbm
    i = pl.program_id(0)
    slot = i % 2

    def drain(s):
        for _ in range(2):
            pltpu.make_async_copy(stage.at[s], xs_hbm.at[pl.ds(0, tm), :], sem.at[s]).wait()

    @pl.when(i >= 2)
    def _():
        drain(slot)

    stage[slot] = h_ref[...]
    for r in range(tm):
        src = stage.at[slot, pl.ds(r, 1), :]
        for k in range(2):
            pltpu.make_async_copy(src, xs_hbm.at[pl.ds(dest_ref[0, k, r], 1), :],
                                  sem.at[slot]).start()

    @pl.when(i == n_tiles - 1)
    def _():
        drain(slot)
        if n_tiles >= 2:
            drain(1 - slot)


def _moe_dispatch(h, dest, n_rows):
    n, d = h.shape
    n_tiles, _, tm = dest.shape
    zeros = jnp.zeros((n_rows, d), h.dtype)
    return pl.pallas_call(
        functools.partial(_dispatch_body, tm=tm, n_tiles=n_tiles),
        out_shape=jax.ShapeDtypeStruct((n_rows, d), h.dtype),
        grid=(n_tiles,),
        in_specs=[pl.BlockSpec((1, META_ROWS, tm), lambda i: (i, 0, 0), memory_space=pltpu.SMEM),
                  pl.BlockSpec((tm, d), lambda i: (i, 0)),
                  pl.BlockSpec(memory_space=pl.ANY)],
        out_specs=pl.BlockSpec(memory_space=pl.ANY),
        scratch_shapes=[pltpu.VMEM((2, tm, d), h.dtype),
                        pltpu.SemaphoreType.DMA((2,))],
        input_output_aliases={2: 0},
        compiler_params=_params(("arbitrary",)),
        name="moe_dispatch",
    )(dest, h, zeros)


def _gmm_body(be_ref, nv_ref, x_ref, wg_ref, wu_ref, wd_ref, y_ref, wg_sc, wu_sc, wd_sc):
    i = pl.program_id(0)
    nv = nv_ref[0]
    new_expert = (i == 0) | (be_ref[i] != be_ref[jnp.maximum(i - 1, 0)])

    @pl.when((i < nv) & new_expert)
    def _():
        wg_sc[...] = wg_ref[0].astype(BF16)
        wu_sc[...] = wu_ref[0].astype(BF16)
        wd_sc[...] = wd_ref[0].astype(BF16)

    @pl.when(i < nv)
    def _():
        x = x_ref[...].astype(BF16)
        gate = jnp.dot(x, wg_sc[...], preferred_element_type=F32)
        up = jnp.dot(x, wu_sc[...], preferred_element_type=F32)
        hid = (gate / (1.0 + jnp.exp(-gate)) * up).astype(BF16)
        y_ref[...] = jnp.dot(hid, wd_sc[...], preferred_element_type=F32)

    @pl.when(i >= nv)
    def _():
        y_ref[...] = jnp.zeros(y_ref.shape, y_ref.dtype)


def _grouped_experts(xs, block_exp, nvalid, wg, wu, wd, tm):
    p, d = xs.shape
    nb = p // tm
    hd = wg.shape[2]
    clamp = lambda i, nv: jnp.minimum(i, nv[0] - 1)
    grid_spec = pltpu.PrefetchScalarGridSpec(
        num_scalar_prefetch=2,
        grid=(nb,),
        in_specs=[
            pl.BlockSpec((tm, d), lambda i, be, nv: (clamp(i, nv), 0)),
            pl.BlockSpec((1, d, hd), lambda i, be, nv: (be[i], 0, 0)),
            pl.BlockSpec((1, d, hd), lambda i, be, nv: (be[i], 0, 0)),
            pl.BlockSpec((1, hd, d), lambda i, be, nv: (be[i], 0, 0)),
        ],
        out_specs=pl.BlockSpec((tm, d), lambda i, be, nv: (i, 0)),
        scratch_shapes=[pltpu.VMEM((d, hd), BF16), pltpu.VMEM((d, hd), BF16),
                        pltpu.VMEM((hd, d), BF16)],
    )
    return pl.pallas_call(
        _gmm_body,
        out_shape=jax.ShapeDtypeStruct((p, d), F32),
        grid_spec=grid_spec,
        compiler_params=_params(("arbitrary",)),
        name="grouped_experts",
    )(block_exp, nvalid, xs, wg, wu, wd)


def _combine_body(pos_cur_ref, pos_nxt_ref, y_hbm, h_ref, route_ref, g_ref, b_ref,
                  *rest, tm, n_tiles, with_bf16):
    if with_bf16:
        out_ref, out_bf_ref, ybuf, sem = rest
    else:
        out_ref, ybuf, sem = rest
    i = pl.program_id(0)
    slot = i % 2

    def start_tile(pos_ref, dst_slot):
        for r in range(tm):
            for k in range(2):
                pltpu.make_async_copy(y_hbm.at[pl.ds(pos_ref[0, k, r], 1), :],
                                      ybuf.at[dst_slot, pl.ds(k * tm + r, 1), :],
                                      sem.at[dst_slot]).start()

    def wait_tile(dst_slot):
        pltpu.make_async_copy(y_hbm.at[pl.ds(0, 2 * tm), :], ybuf.at[dst_slot],
                              sem.at[dst_slot]).wait()

    @pl.when(i == 0)
    def _():
        start_tile(pos_cur_ref, 0)

    start_tile(pos_nxt_ref, 1 - slot)
    wait_tile(slot)
    y1 = ybuf[slot, 0:tm, :]
    y2 = ybuf[slot, tm:2 * tm, :]
    route = route_ref[...]
    ffn = route[:, 2:3] * y1 + route[:, 3:4] * y2
    hn = _layer_norm(DN_ALPHA * h_ref[...] + ffn, g_ref[...], b_ref[...])
    out_ref[...] = hn
    if with_bf16:
        out_bf_ref[...] = hn.astype(BF16)

    @pl.when(i == n_tiles - 1)
    def _():
        wait_tile(1 - slot)


def _combine_ln(y_sorted, dest, h, route, ln_g, ln_b, with_bf16):
    n, d = h.shape
    n_tiles, _, tm = dest.shape
    const = lambda i: (0, 0)
    out_shape = [jax.ShapeDtypeStruct((n, d), F32)]
    out_specs = [pl.BlockSpec((tm, d), lambda i: (i, 0))]
    if with_bf16:
        out_shape.append(jax.ShapeDtypeStruct((n, d), BF16))
        out_specs.append(pl.BlockSpec((tm, d), lambda i: (i, 0)))
    return pl.pallas_call(
        functools.partial(_combine_body, tm=tm, n_tiles=n_tiles, with_bf16=with_bf16),
        out_shape=tuple(out_shape),
        grid=(n_tiles,),
        in_specs=[
            pl.BlockSpec((1, META_ROWS, tm), lambda i: (i, 0, 0), memory_space=pltpu.SMEM),
            pl.BlockSpec((1, META_ROWS, tm), lambda i: (jnp.minimum(i + 1, n_tiles - 1), 0, 0),
                         memory_space=pltpu.SMEM),
            pl.BlockSpec(memory_space=pl.ANY),
            pl.BlockSpec((tm, d), lambda i: (i, 0)),
            pl.BlockSpec((tm, LANES), lambda i: (i, 0)),
            pl.BlockSpec((1, d), const),
            pl.BlockSpec((1, d), const),
        ],
        out_specs=tuple(out_specs),
        scratch_shapes=[pltpu.VMEM((2, 2 * tm, d), y_sorted.dtype),
                        pltpu.SemaphoreType.DMA((2,))],
        compiler_params=_params(("arbitrary",)),
        name="moe_combine_ln",
    )(dest, dest, y_sorted, h, route, ln_g.reshape(1, d), ln_b.reshape(1, d))


def _hier_moe_ln(h, route, w_gate, w_up, w_down, ln_g, ln_b, with_bf16):
    n = h.shape[0]
    tile = MOE_TILE
    nb = 2 * n // tile + MOE_EXPERTS
    dest, meta = _moe_plan(route, tile)
    pends = jnp.cumsum(meta[1, :MOE_EXPERTS])
    nvalid = (pends[-1] // tile).astype(I32)
    bstart = jnp.arange(nb, dtype=I32) * tile
    block_exp = jnp.sum((pends[None, :] <= bstart[:, None]).astype(I32), axis=1)
    block_exp = jnp.minimum(block_exp, MOE_EXPERTS - 1)
    block_exp = jnp.where(jnp.arange(nb) < nvalid, block_exp, block_exp[nvalid - 1])
    xs = _moe_dispatch(h, dest, nb * tile)
    y_sorted = _grouped_experts(xs, block_exp, nvalid.reshape(1), w_gate, w_up, w_down, tile)
    return _combine_ln(y_sorted, dest, h, route, ln_g, ln_b, with_bf16)


def _gelu_tanh(x):
    c = math.sqrt(2.0 / math.pi)
    return 0.5 * x * (1.0 + jnp.tanh(c * (x + 0.044715 * (x * x * x))))


def _sg_body(x_ref, w_ref, b_ref, lng_ref, lnb_ref, ws_ref, bs_ref, o_ref, *, half, gdim):
    x = x_ref[...]
    v = _gelu_tanh(jnp.dot(x, w_ref[:, half:], preferred_element_type=F32) + b_ref[:, half:])
    vn = _layer_norm(v, lng_ref[...], lnb_ref[...]).astype(BF16)
    for g in range(SG_GROUPS):
        cols = slice(g * gdim, (g + 1) * gdim)
        u = _gelu_tanh(jnp.dot(x, w_ref[:, cols], preferred_element_type=F32) + b_ref[:, cols])
        s = jnp.dot(ws_ref[g], vn[:, cols], preferred_element_type=F32) + bs_ref[:, g:g + 1]
        o_ref[:, cols] = (u * s).astype(o_ref.dtype)


def _spatial_gating(x_bf16, w_in, b_in, ln_g, ln_b, w_s, b_s):
    n, d = x_bf16.shape
    half = w_in.shape[1] // 2
    gdim = half // SG_GROUPS
    tm = SG_TILE
    reps = tm // SG_CHUNK
    causal = np.tril(np.ones((SG_CHUNK, SG_CHUNK), dtype=bool))
    w_tri = jnp.where(causal[None], w_s, 0.0).astype(BF16)
    eye = jnp.eye(reps, dtype=BF16)
    w_blk = jnp.einsum("ab,gts->gatbs", eye, w_tri).reshape(SG_GROUPS, tm, tm)
    bs_tile = jnp.tile(jnp.transpose(b_s), (reps, 1))
    const2 = lambda i: (0, 0)
    return pl.pallas_call(
        functools.partial(_sg_body, half=half, gdim=gdim),
        out_shape=jax.ShapeDtypeStruct((n, half), BF16),
        grid=(n // tm,),
        in_specs=[pl.BlockSpec((tm, d), lambda i: (i, 0)),
                  pl.BlockSpec((d, 2 * half), const2),
                  pl.BlockSpec((1, 2 * half), const2),
                  pl.BlockSpec((1, half), const2),
                  pl.BlockSpec((1, half), const2),
                  pl.BlockSpec((SG_GROUPS, tm, tm), lambda i: (0, 0, 0)),
                  pl.BlockSpec((tm, SG_GROUPS), const2)],
        out_specs=pl.BlockSpec((tm, half), lambda i: (i, 0)),
        compiler_params=_params(("parallel",)),
        name="spatial_gating",
    )(x_bf16, w_in.astype(BF16), b_in.reshape(1, -1), ln_g.reshape(1, -1),
      ln_b.reshape(1, -1), w_blk, bs_tile)


def kernel(x, rel_bias, attn_w_in, attn_w_out, attn_lambda, attn_subln_g, sg_w_in, sg_b_in,
           sg_ln_g, sg_ln_b, sg_w_s, sg_b_s, sg_w_out, moe_w_group, moe_b_group, moe_w_router,
           moe_b_router, moe_w_gate, moe_w_up, moe_w_down, ln_g, ln_b):
    b, s, d = x.shape
    n = b * s
    h = x.reshape(n, d)

    lam_init = 0.8 - 0.6 * math.exp(-0.3 * 0)
    t = min(ATTN_TILE, s)
    qkv = _qkv_proj(h, attn_w_in[0].astype(BF16)).reshape(b, s, 3 * d)
    bias_tiles = _bias_tiles(rel_bias, t, s)
    o = _diff_attention(qkv, bias_tiles, attn_lambda[0], attn_subln_g[0], lam_init, t)
    wr, br = _router_params(moe_w_group[0], moe_b_group[0], moe_w_router[0], moe_b_router[0])
    h, route = _proj_ln_route(o.reshape(n, d), attn_w_out[0].astype(BF16), h,
                              ln_g[0, 0], ln_b[0, 0], wr, br)
    h, h_bf16 = _hier_moe_ln(h, route, moe_w_gate[0], moe_w_up[0], moe_w_down[0],
                             ln_g[0, 1], ln_b[0, 1], with_bf16=True)

    gated = _spatial_gating(h_bf16, sg_w_in[0], sg_b_in[0], sg_ln_g[0], sg_ln_b[0],
                            sg_w_s[0], sg_b_s[0])
    wr, br = _router_params(moe_w_group[1], moe_b_group[1], moe_w_router[1], moe_b_router[1])
    h, route = _proj_ln_route(gated, sg_w_out[0].astype(BF16), h,
                              ln_g[1, 0], ln_b[1, 0], wr, br)
    (h,) = _hier_moe_ln(h, route, moe_w_gate[1], moe_w_up[1], moe_w_down[1],
                        ln_g[1, 1], ln_b[1, 1], with_bf16=False)
    return h.reshape(b, s, d)
```

```python
import functools
import math

import numpy as np
import jax
import jax.numpy as jnp
from jax import lax
from jax.experimental import pallas as pl
from jax.experimental.pallas import tpu as pltpu

F32 = jnp.float32
BF16 = jnp.bfloat16
I32 = jnp.int32

DA_HEADS = 8
DA_HEAD_DIM = 64
DA_VDIM = 2 * DA_HEAD_DIM
REL_BUCKETS = 32
REL_MAX_DIST = 128
SG_CHUNK = 128
SG_GROUPS = 8
MOE_GROUPS = 8
MOE_PER_GROUP = 8
MOE_EXPERTS = MOE_GROUPS * MOE_PER_GROUP
LN_EPS = 1e-5
DEPTH = 2
DN_ALPHA = (2 * DEPTH) ** 0.25

LANES = 128
NEG_BIG = -1e30
LOG2E = math.log2(math.e)
VMEM_LIMIT = 52 * 1024 * 1024

ROW_TILE = 512
ATTN_TILE = 512
ATTN_HEADS = 2
VT_ROWS = DA_VDIM + 16
SG_TILE = 2 * SG_CHUNK
MOE_TILE = 512
TOKEN_TILE = 256
PLAN_TILE = 512
ROW_PIECES = 8
META_ROWS = 8


def _params(sem):
    return pltpu.CompilerParams(dimension_semantics=sem, vmem_limit_bytes=VMEM_LIMIT)


def _layer_norm(y, g, b):
    mu = jnp.mean(y, axis=-1, keepdims=True)
    yc = y - mu
    var = jnp.mean(yc * yc, axis=-1, keepdims=True)
    return yc * lax.rsqrt(var + LN_EPS) * g + b


def _qkv_body(x_ref, w_ref, o_ref, *, d_model, q_scale):
    x = x_ref[...].astype(BF16)
    for c in range(3):
        cols = slice(c * d_model, (c + 1) * d_model)
        acc = jnp.dot(x, w_ref[:, cols], preferred_element_type=F32)
        if c == 0:
            acc = acc * q_scale
        o_ref[:, cols] = acc.astype(BF16)


def _qkv_proj(x2d, w_bf16):
    n, d = x2d.shape
    tm = min(ROW_TILE, n)
    return pl.pallas_call(
        functools.partial(_qkv_body, d_model=d, q_scale=DA_HEAD_DIM ** -0.5 * LOG2E),
        out_shape=jax.ShapeDtypeStruct((n, 3 * d), BF16),
        grid=(n // tm,),
        in_specs=[pl.BlockSpec((tm, d), lambda i: (i, 0)),
                  pl.BlockSpec((d, 3 * d), lambda i: (0, 0))],
        out_specs=pl.BlockSpec((tm, 3 * d), lambda i: (i, 0)),
        compiler_params=_params(("parallel",)),
        name="qkv_proj",
    )(x2d, w_bf16)


def _t5_bucket_np(dist):
    n = np.maximum(dist, 0)
    max_exact = REL_BUCKETS // 2
    nf = np.maximum(n, 1).astype(np.float64)
    large = max_exact + (np.log(nf / max_exact) / math.log(REL_MAX_DIST / max_exact)
                         * (REL_BUCKETS - max_exact)).astype(np.int64)
    large = np.minimum(large, REL_BUCKETS - 1)
    return np.where(n < max_exact, n, large)


def _bucket_tiles(t, seq):
    assert np.all(_t5_bucket_np(np.arange(t + 1, max(seq, t + 2))) == REL_BUCKETS - 1)
    kk = np.arange(t)[:, None]
    qq = np.arange(t)[None, :]
    diag = np.where(qq >= kk, _t5_bucket_np(qq - kk), -1)
    sub = _t5_bucket_np(t + qq - kk)
    return np.stack([diag, sub]).astype(np.int32)


def _bias_body(tbl_ref, bucket_ref, o_ref):
    h = pl.program_id(0)
    last = tbl_ref[REL_BUCKETS - 1, h]
    bk = bucket_ref[...]
    acc = jnp.zeros(bk.shape, F32)
    for b in range(REL_BUCKETS - 1):
        acc = jnp.where(bk == b, (tbl_ref[b, h] - last) * LOG2E, acc)
    o_ref[0] = jnp.where(bk < 0, NEG_BIG, acc)


def _bias_tiles(rel_bias, t, seq):
    heads = rel_bias.shape[1]
    buckets = jnp.asarray(_bucket_tiles(t, seq))
    return pl.pallas_call(
        _bias_body,
        out_shape=jax.ShapeDtypeStruct((heads, 2, t, t), F32),
        grid=(heads,),
        in_specs=[pl.BlockSpec(memory_space=pltpu.SMEM),
                  pl.BlockSpec((2, t, t), lambda h: (0, 0, 0))],
        out_specs=pl.BlockSpec((1, 2, t, t), lambda h: (h, 0, 0, 0)),
        compiler_params=_params(("arbitrary",)),
        name="rel_bias_tiles",
    )(rel_bias, buckets)


def _attn_body(q_ref, k_ref, v_ref, bias_ref, lam_ref, g_ref, o_ref, vt_sc, *, t, lam_init):
    i = pl.program_id(2)
    n_heads = vt_sc.shape[0]

    @pl.when(i == 0)
    def _():
        for g in range(n_heads):
            for c in range(vt_sc.shape[1]):
                v = v_ref[0, c * t:(c + 1) * t, g * LANES:(g + 1) * LANES]
                vt_sc[g, c, 0:DA_VDIM, :] = v.astype(F32).T.astype(BF16)
                vt_sc[g, c, DA_VDIM:VT_ROWS, :] = jnp.ones((VT_ROWS - DA_VDIM, t), BF16)

    q2s = []
    for g in range(n_heads):
        q = q_ref[0, :, g * LANES:(g + 1) * LANES]
        lane = lax.broadcasted_iota(I32, q.shape, 1)
        zero = jnp.zeros_like(q)
        q2s.append(jnp.concatenate([jnp.where(lane < DA_HEAD_DIM, q, zero),
                                    jnp.where(lane >= DA_HEAD_DIM, q, zero)], axis=0))

    def block(j, bias_idx, state):
        start = pl.multiple_of(j * t, t)
        scores = []
        for g in range(n_heads):
            k = k_ref[0, pl.ds(start, t), g * LANES:(g + 1) * LANES]
            st = lax.dot_general(k, q2s[g], (((1,), (1,)), ((), ())),
                                 preferred_element_type=F32)
            if bias_idx is not None:
                bias = bias_ref[g, bias_idx]
                st = st + jnp.concatenate([bias, bias], axis=1)
            scores.append(st)
        out = []
        for g in range(n_heads):
            m_prev, acc_prev = state[g]
            st = scores[g]
            m_new = jnp.maximum(m_prev, jnp.max(st, axis=0, keepdims=True))
            alpha = jnp.exp2(m_prev - m_new)
            p = jnp.exp2(st - m_new).astype(BF16)
            acc_new = alpha * acc_prev + jnp.dot(vt_sc[g, j], p,
                                                 preferred_element_type=F32)
            out.append((m_new, acc_new))
        return tuple(out)

    state = tuple((jnp.full((1, 2 * t), NEG_BIG, F32),
                   jnp.zeros((VT_ROWS, 2 * t), F32)) for _ in range(n_heads))
    state = lax.fori_loop(0, jnp.maximum(i - 1, 0), lambda j, st: block(j, None, st), state)
    state = lax.cond(i >= 1, lambda st: block(i - 1, 1, st), lambda st: st, state)
    state = block(i, 0, state)

    lv = lam_ref[...]
    dot_a = jnp.sum(lv[0:1] * lv[1:2], axis=-1, keepdims=True)
    dot_b = jnp.sum(lv[2:3] * lv[3:4], axis=-1, keepdims=True)
    lam = jnp.exp(dot_a) - jnp.exp(dot_b) + lam_init

    for g in range(n_heads):
        _, acc_fin = state[g]
        o_all = acc_fin[0:DA_VDIM] / acc_fin[DA_VDIM:DA_VDIM + 1]
        o = o_all[:, :t] - lam * o_all[:, t:]
        ms = jnp.mean(o * o, axis=0, keepdims=True)
        o = o * lax.rsqrt(ms + LN_EPS) * g_ref[...] * (1.0 - lam_init)
        o_ref[0, :, g * LANES:(g + 1) * LANES] = o.T.astype(o_ref.dtype)


def _diff_attention(qkv, bias_tiles, lam_vecs, subln_g, lam_init, t):
    b, s, d3 = qkv.shape
    d = d3 // 3
    heads = DA_HEADS
    assert d == heads * DA_VDIM and DA_VDIM == LANES and s % t == 0 and heads % ATTN_HEADS == 0
    hg = heads // ATTN_HEADS
    w = ATTN_HEADS * LANES
    return pl.pallas_call(
        functools.partial(_attn_body, t=t, lam_init=lam_init),
        out_shape=jax.ShapeDtypeStruct((b, s, d), BF16),
        grid=(b, hg, s // t),
        in_specs=[
            pl.BlockSpec((1, t, w), lambda bi, h, i: (bi, i, h)),
            pl.BlockSpec((1, s, w), lambda bi, h, i: (bi, 0, hg + h)),
            pl.BlockSpec((1, s, w), lambda bi, h, i: (bi, 0, 2 * hg + h)),
            pl.BlockSpec((ATTN_HEADS, 2, t, t), lambda bi, h, i: (h, 0, 0, 0)),
            pl.BlockSpec((4, DA_HEAD_DIM), lambda bi, h, i: (0, 0)),
            pl.BlockSpec((DA_VDIM, 1), lambda bi, h, i: (0, 0)),
        ],
        out_specs=pl.BlockSpec((1, t, w), lambda bi, h, i: (bi, i, h)),
        scratch_shapes=[pltpu.VMEM((ATTN_HEADS, s // t, VT_ROWS, t), BF16)],
        compiler_params=_params(("parallel", "parallel", "arbitrary")),
        name="diff_attention",
    )(qkv, qkv, qkv, bias_tiles, lam_vecs, subln_g.reshape(DA_VDIM, 1))


def _route(logits):
    lane = lax.broadcasted_iota(I32, logits.shape, 1)
    lane_f = lane.astype(F32)
    far = float(2 * LANES)
    is_g = lane < MOE_GROUPS
    gl = jnp.where(is_g, logits, NEG_BIG)
    gmax = jnp.max(gl, axis=-1, keepdims=True)
    gidx = jnp.min(jnp.where(gl == gmax, lane_f, far), axis=-1, keepdims=True)
    gsum = jnp.sum(jnp.where(is_g, jnp.exp(gl - gmax), 0.0), axis=-1, keepdims=True)
    g_gate = 1.0 / gsum
    lo = MOE_GROUPS + MOE_PER_GROUP * gidx
    in_grp = (lane_f >= lo) & (lane_f < lo + MOE_PER_GROUP)
    el = jnp.where(in_grp, logits, NEG_BIG)
    v1 = jnp.max(el, axis=-1, keepdims=True)
    i1 = jnp.min(jnp.where(el == v1, lane_f, far), axis=-1, keepdims=True)
    el2 = jnp.where(lane_f == i1, NEG_BIG, el)
    v2 = jnp.max(el2, axis=-1, keepdims=True)
    i2 = jnp.min(jnp.where(el2 == v2, lane_f, far), axis=-1, keepdims=True)
    e21 = jnp.exp(v2 - v1)
    w1 = g_gate / (1.0 + e21)
    w2 = g_gate * e21 / (1.0 + e21)
    out = jnp.where(lane == 0, i1 - MOE_GROUPS, 0.0)
    out = jnp.where(lane == 1, i2 - MOE_GROUPS, out)
    out = jnp.where(lane == 2, w1, out)
    out = jnp.where(lane == 3, w2, out)
    return out


def _proj_ln_body(a_ref, w_ref, h_ref, g_ref, b_ref, wr_ref, br_ref, hout_ref, route_ref, cnt_ref):
    mix = jnp.dot(a_ref[...], w_ref[...], preferred_element_type=F32)
    hn = _layer_norm(DN_ALPHA * h_ref[...] + mix, g_ref[...], b_ref[...])
    hout_ref[...] = hn
    x_hi = hn.astype(BF16)
    x_lo = (hn - x_hi.astype(F32)).astype(BF16)
    hi_part = jnp.dot(x_hi, wr_ref[...], preferred_element_type=F32)
    lo_part = jnp.dot(x_lo, wr_ref[:, :LANES], preferred_element_type=F32)
    logits = hi_part[:, :LANES] + (hi_part[:, LANES:] + lo_part) + br_ref[...]
    route = _route(logits)
    route_ref[...] = route

    @pl.when(pl.program_id(0) == 0)
    def _():
        cnt_ref[...] = jnp.zeros(cnt_ref.shape, F32)

    lane = lax.broadcasted_iota(I32, route.shape, 1).astype(F32)
    picked = jnp.where(lane == route[:, 0:1], 1.0, 0.0) + jnp.where(lane == route[:, 1:2], 1.0, 0.0)
    cnt_ref[...] += jnp.sum(picked, axis=0, keepdims=True)


def _proj_ln_route(a, w_bf16, h, ln_g, ln_b, w_route, b_route):
    n, k = a.shape
    d = w_bf16.shape[1]
    tm = min(ROW_TILE, n)
    const = lambda i: (0, 0)
    return pl.pallas_call(
        _proj_ln_body,
        out_shape=(jax.ShapeDtypeStruct((n, d), F32),
                   jax.ShapeDtypeStruct((n, LANES), F32),
                   jax.ShapeDtypeStruct((META_ROWS, LANES), F32)),
        grid=(n // tm,),
        in_specs=[pl.BlockSpec((tm, k), lambda i: (i, 0)),
                  pl.BlockSpec((k, d), const),
                  pl.BlockSpec((tm, d), lambda i: (i, 0)),
                  pl.BlockSpec((1, d), const),
                  pl.BlockSpec((1, d), const),
                  pl.BlockSpec((d, 2 * LANES), const),
                  pl.BlockSpec((1, LANES), const)],
        out_specs=(pl.BlockSpec((tm, d), lambda i: (i, 0)),
                   pl.BlockSpec((tm, LANES), lambda i: (i, 0)),
                   pl.BlockSpec((META_ROWS, LANES), const)),
        compiler_params=_params(("arbitrary",)),
        name="proj_ln_route",
    )(a, w_bf16, h, ln_g.reshape(1, d), ln_b.reshape(1, d), w_route, b_route)


def _router_params(w_group, b_group, w_router, b_router):
    d = w_group.shape[0]
    pad = LANES - MOE_GROUPS - MOE_EXPERTS
    w = jnp.concatenate([w_group, w_router, jnp.zeros((d, pad), F32)], axis=1)
    b = jnp.concatenate([b_group, b_router, jnp.zeros((pad,), F32)]).reshape(1, LANES)
    w_hi = w.astype(BF16)
    w_lo = (w - w_hi.astype(F32)).astype(BF16)
    return jnp.concatenate([w_hi, w_lo], axis=1), b


def _plan_body(route_ref, cnt_ref, dest_ref, meta_ref, carry_sc, *, pt, tm, moe_tile):
    i = pl.program_id(0)

    @pl.when(i == 0)
    def _():
        cnt = cnt_ref[0:1, :]
        padded = jnp.floor((cnt + (moe_tile - 1)) * (1.0 / moe_tile)) * moe_tile
        r = lax.broadcasted_iota(I32, (LANES, LANES), 0)
        c = lax.broadcasted_iota(I32, (LANES, LANES), 1)
        upper = jnp.where(r < c, 1.0, 0.0)
        pstart = jnp.dot(jnp.broadcast_to(padded, (META_ROWS, LANES)), upper,
                         precision=lax.Precision.HIGHEST, preferred_element_type=F32)[0:1]
        carry_sc[...] = pstart
        row = lax.broadcasted_iota(I32, (META_ROWS, LANES), 0)
        meta = jnp.where(row == 0, cnt, jnp.where(row == 1, padded,
                                                  jnp.where(row == 2, pstart, 0.0)))
        meta_ref[...] = meta.astype(I32)

    route = route_ref[...]
    lane_i = lax.broadcasted_iota(I32, route.shape, 1)
    lane = lane_i.astype(F32)
    f1 = jnp.where(lane == route[:, 0:1], 1.0, 0.0)
    f2 = jnp.where(lane == route[:, 1:2], 1.0, 0.0)
    tot1 = jnp.sum(f1, axis=0, keepdims=True)
    tot2 = jnp.sum(f2, axis=0, keepdims=True)
    carry = carry_sc[...]
    rr = lax.broadcasted_iota(I32, (pt, pt), 0)
    cc = lax.broadcasted_iota(I32, (pt, pt), 1)
    lower = jnp.where(rr > cc, 1.0, 0.0).astype(BF16)
    pre1 = jnp.dot(lower, f1.astype(BF16), preferred_element_type=F32)
    pre2 = jnp.dot(lower, f2.astype(BF16), preferred_element_type=F32)
    d1 = jnp.sum(f1 * (carry + pre1), axis=1, keepdims=True)
    d2 = jnp.sum(f2 * (carry + tot1 + pre2), axis=1, keepdims=True)
    carry_sc[...] = carry + tot1 + tot2
    both = jnp.where(lane_i == 0, d1, jnp.where(lane_i == 1, d2, 0.0))
    both_t = both.T[0:META_ROWS].astype(I32)
    for c in range(pt // tm):
        dest_ref[c] = both_t[:, c * tm:(c + 1) * tm]


def _moe_plan(route, counts, moe_tile):
    n = route.shape[0]
    tm = min(TOKEN_TILE, n)
    pt = min(PLAN_TILE, n)
    return pl.pallas_call(
        functools.partial(_plan_body, pt=pt, tm=tm, moe_tile=moe_tile),
        out_shape=(jax.ShapeDtypeStruct((n // tm, META_ROWS, tm), I32),
                   jax.ShapeDtypeStruct((META_ROWS, LANES), I32)),
        grid=(n // pt,),
        in_specs=[pl.BlockSpec((pt, LANES), lambda i: (i, 0)),
                  pl.BlockSpec((META_ROWS, LANES), lambda i: (0, 0))],
        out_specs=(pl.BlockSpec((pt // tm, META_ROWS, tm), lambda i: (i, 0, 0)),
                   pl.BlockSpec((META_ROWS, LANES), lambda i: (0, 0))),
        scratch_shapes=[pltpu.VMEM((1, LANES), F32)],
        compiler_params=_params(("arbitrary",)),
        name="moe_plan",
    )(route, counts)


def _rows_to_tiled(dst_ref, mat, n_rows):
    for s in range(ROW_PIECES):
        dst_ref[pl.ds(s, n_rows, stride=ROW_PIECES), :] = mat[:, s * LANES:(s + 1) * LANES]


def _tiled_to_rows(src_ref, first, n_rows):
    return jnp.concatenate([src_ref[pl.ds(first * ROW_PIECES + s, n_rows, stride=ROW_PIECES), :]
                            for s in range(ROW_PIECES)], axis=1)


def _dispatch_body(dest_ref, meta_ref, h_ref, xs_hbm, stage, zero_sc, sem, zsem,
                   *, tm, n_tiles, moe_tile, n_rows):
    i = pl.program_id(0)
    slot = i % 2
    tile_rows = moe_tile * ROW_PIECES

    def zero_tile(first_row):
        start = pl.multiple_of(first_row * ROW_PIECES, tile_rows)
        return pltpu.make_async_copy(zero_sc, xs_hbm.at[pl.ds(start, tile_rows), :], zsem)

    def zero_fills(act):
        last = MOE_EXPERTS - 1
        total = meta_ref[2, last] + meta_ref[1, last]
        for e in range(MOE_EXPERTS):
            @pl.when(meta_ref[1, e] > 0)
            def _():
                act(zero_tile(meta_ref[2, e] + meta_ref[1, e] - moe_tile))

            @pl.when(total + e * moe_tile < n_rows)
            def _():
                act(zero_tile(total + e * moe_tile))

    @pl.when(i == 0)
    def _():
        zero_sc[...] = jnp.zeros(zero_sc.shape, zero_sc.dtype)
        zero_fills(lambda cp: cp.start())
        zero_fills(lambda cp: cp.wait())

    def drain(s):
        for _ in range(2):
            pltpu.make_async_copy(stage.at[s], xs_hbm.at[pl.ds(0, tm * ROW_PIECES), :],
                                  sem.at[s]).wait()

    @pl.when(i >= 2)
    def _():
        drain(slot)

    _rows_to_tiled(stage.at[slot], h_ref[...], tm)
    for r in range(tm):
        src = stage.at[slot, pl.ds(r * ROW_PIECES, ROW_PIECES), :]
        for k in range(2):
            row = pl.multiple_of(dest_ref[0, k, r] * ROW_PIECES, ROW_PIECES)
            pltpu.make_async_copy(src, xs_hbm.at[pl.ds(row, ROW_PIECES), :],
                                  sem.at[slot]).start(priority=k)

    @pl.when(i == n_tiles - 1)
    def _():
        drain(slot)
        if n_tiles >= 2:
            drain(1 - slot)


def _moe_dispatch(h, dest, meta, n_rows, moe_tile):
    n, d = h.shape
    n_tiles, _, tm = dest.shape
    assert d == ROW_PIECES * LANES
    return pl.pallas_call(
        functools.partial(_dispatch_body, tm=tm, n_tiles=n_tiles, moe_tile=moe_tile,
                          n_rows=n_rows),
        out_shape=jax.ShapeDtypeStruct((n_rows * ROW_PIECES, LANES), h.dtype),
        grid=(n_tiles,),
        in_specs=[pl.BlockSpec((1, META_ROWS, tm), lambda i: (i, 0, 0), memory_space=pltpu.SMEM),
                  pl.BlockSpec(memory_space=pltpu.SMEM),
                  pl.BlockSpec((tm, d), lambda i: (i, 0))],
        out_specs=pl.BlockSpec(memory_space=pl.ANY),
        scratch_shapes=[pltpu.VMEM((2, tm * ROW_PIECES, LANES), h.dtype),
                        pltpu.VMEM((moe_tile * ROW_PIECES, LANES), h.dtype),
                        pltpu.SemaphoreType.DMA((2,)),
                        pltpu.SemaphoreType.DMA(())],
        compiler_params=_params(("arbitrary",)),
        name="moe_dispatch",
    )(dest, meta, h)


def _gmm_body(be_ref, nv_ref, x_ref, wg_ref, wu_ref, wd_ref, y_ref, wg_sc, wu_sc, wd_sc, *, tm):
    i = pl.program_id(0)
    nv = nv_ref[0]
    new_expert = (i == 0) | (be_ref[i] != be_ref[jnp.maximum(i - 1, 0)])

    @pl.when((i < nv) & new_expert)
    def _():
        wg_sc[...] = wg_ref[0, 0].astype(BF16)
        wu_sc[...] = wu_ref[0, 0].astype(BF16)
        wd_sc[...] = wd_ref[0, 0].astype(BF16)

    @pl.when(i < nv)
    def _():
        x = _tiled_to_rows(x_ref, 0, tm).astype(BF16)
        gate = jnp.dot(x, wg_sc[...], preferred_element_type=F32)
        up = jnp.dot(x, wu_sc[...], preferred_element_type=F32)
        hid = (gate / (1.0 + jnp.exp(-gate)) * up).astype(BF16)
        _rows_to_tiled(y_ref, jnp.dot(hid, wd_sc[...], preferred_element_type=F32), tm)

    @pl.when(i >= nv)
    def _():
        y_ref[...] = jnp.zeros(y_ref.shape, y_ref.dtype)


def _grouped_experts(xs, block_exp, nvalid, wg, wu, wd, layer, tm):
    nb = xs.shape[0] // (tm * ROW_PIECES)
    d, hd = wg.shape[2], wg.shape[3]
    clamp = lambda i, nv: jnp.minimum(i, nv[0] - 1)
    grid_spec = pltpu.PrefetchScalarGridSpec(
        num_scalar_prefetch=2,
        grid=(nb,),
        in_specs=[
            pl.BlockSpec((tm * ROW_PIECES, LANES), lambda i, be, nv: (clamp(i, nv), 0)),
            pl.BlockSpec((1, 1, d, hd), lambda i, be, nv: (layer, be[i], 0, 0)),
            pl.BlockSpec((1, 1, d, hd), lambda i, be, nv: (layer, be[i], 0, 0)),
            pl.BlockSpec((1, 1, hd, d), lambda i, be, nv: (layer, be[i], 0, 0)),
        ],
        out_specs=pl.BlockSpec((tm * ROW_PIECES, LANES), lambda i, be, nv: (i, 0)),
        scratch_shapes=[pltpu.VMEM((d, hd), BF16), pltpu.VMEM((d, hd), BF16),
                        pltpu.VMEM((hd, d), BF16)],
    )
    return pl.pallas_call(
        functools.partial(_gmm_body, tm=tm),
        out_shape=jax.ShapeDtypeStruct(xs.shape, F32),
        grid_spec=grid_spec,
        compiler_params=_params(("arbitrary",)),
        name="grouped_experts",
    )(block_exp, nvalid, xs, wg, wu, wd)


def _combine_body(pos_cur_ref, pos_nxt_ref, y_hbm, h_ref, route_ref, g_ref, b_ref,
                  *rest, tm, n_tiles, with_bf16):
    if with_bf16:
        out_ref, out_bf_ref, ybuf, sem = rest
    else:
        out_ref, ybuf, sem = rest
    i = pl.program_id(0)
    slot = i % 2

    def start_tile(pos_ref, dst_slot):
        for r in range(tm):
            for k in range(2):
                row = pl.multiple_of(pos_ref[0, k, r] * ROW_PIECES, ROW_PIECES)
                pltpu.make_async_copy(
                    y_hbm.at[pl.ds(row, ROW_PIECES), :],
                    ybuf.at[dst_slot, pl.ds((k * tm + r) * ROW_PIECES, ROW_PIECES), :],
                    sem.at[dst_slot]).start(priority=k)

    def wait_tile(dst_slot):
        pltpu.make_async_copy(y_hbm.at[pl.ds(0, 2 * tm * ROW_PIECES), :], ybuf.at[dst_slot],
                              sem.at[dst_slot]).wait()

    @pl.when(i == 0)
    def _():
        start_tile(pos_cur_ref, 0)

    start_tile(pos_nxt_ref, 1 - slot)
    wait_tile(slot)
    y1 = _tiled_to_rows(ybuf.at[slot], 0, tm)
    y2 = _tiled_to_rows(ybuf.at[slot], tm, tm)
    route = route_ref[...]
    ffn = route[:, 2:3] * y1 + route[:, 3:4] * y2
    hn = _layer_norm(DN_ALPHA * h_ref[...] + ffn, g_ref[...], b_ref[...])
    out_ref[...] = hn
    if with_bf16:
        out_bf_ref[...] = hn.astype(BF16)

    @pl.when(i == n_tiles - 1)
    def _():
        wait_tile(1 - slot)


def _combine_ln(y_sorted, dest, h, route, ln_g, ln_b, with_bf16):
    n, d = h.shape
    n_tiles, _, tm = dest.shape
    const = lambda i: (0, 0)
    out_shape = [jax.ShapeDtypeStruct((n, d), F32)]
    out_specs = [pl.BlockSpec((tm, d), lambda i: (i, 0))]
    if with_bf16:
        out_shape.append(jax.ShapeDtypeStruct((n, d), BF16))
        out_specs.append(pl.BlockSpec((tm, d), lambda i: (i, 0)))
    return pl.pallas_call(
        functools.partial(_combine_body, tm=tm, n_tiles=n_tiles, with_bf16=with_bf16),
        out_shape=tuple(out_shape),
        grid=(n_tiles,),
        in_specs=[
            pl.BlockSpec((1, META_ROWS, tm), lambda i: (i, 0, 0), memory_space=pltpu.SMEM),
            pl.BlockSpec((1, META_ROWS, tm), lambda i: (jnp.minimum(i + 1, n_tiles - 1), 0, 0),
                         memory_space=pltpu.SMEM),
            pl.BlockSpec(memory_space=pl.ANY),
            pl.BlockSpec((tm, d), lambda i: (i, 0)),
            pl.BlockSpec((tm, LANES), lambda i: (i, 0)),
            pl.BlockSpec((1, d), const),
            pl.BlockSpec((1, d), const),
        ],
        out_specs=tuple(out_specs),
        scratch_shapes=[pltpu.VMEM((2, 2 * tm * ROW_PIECES, LANES), y_sorted.dtype),
                        pltpu.SemaphoreType.DMA((2,))],
        compiler_params=_params(("arbitrary",)),
        name="moe_combine_ln",
    )(dest, dest, y_sorted, h, route, ln_g.reshape(1, d), ln_b.reshape(1, d))


def _hier_moe_ln(h, route, counts, w_gate, w_up, w_down, layer, ln_g, ln_b, with_bf16):
    n = h.shape[0]
    tile = MOE_TILE
    nb = 2 * n // tile + MOE_EXPERTS
    dest, meta = _moe_plan(route, counts, tile)
    pends = jnp.cumsum(meta[1, :MOE_EXPERTS])
    nvalid = (pends[-1] // tile).astype(I32)
    bstart = jnp.arange(nb, dtype=I32) * tile
    block_exp = jnp.sum((pends[None, :] <= bstart[:, None]).astype(I32), axis=1)
    block_exp = jnp.minimum(block_exp, MOE_EXPERTS - 1)
    block_exp = jnp.where(jnp.arange(nb) < nvalid, block_exp, block_exp[nvalid - 1])
    xs = _moe_dispatch(h, dest, meta, nb * tile, tile)
    y_sorted = _grouped_experts(xs, block_exp, nvalid.reshape(1), w_gate, w_up, w_down, layer, tile)
    return _combine_ln(y_sorted, dest, h, route, ln_g, ln_b, with_bf16)


def _gelu_tanh(x):
    c = math.sqrt(2.0 / math.pi)
    return 0.5 * x * (1.0 + jnp.tanh(c * (x + 0.044715 * (x * x * x))))


def _sg_body(x_ref, w_ref, b_ref, lng_ref, lnb_ref, ws_ref, bs_ref, o_ref, *, half, gdim):
    x = x_ref[...]
    v = _gelu_tanh(jnp.dot(x, w_ref[:, half:], preferred_element_type=F32) + b_ref[:, half:])
    vn = _layer_norm(v, lng_ref[...], lnb_ref[...]).astype(BF16)
    for g in range(SG_GROUPS):
        cols = slice(g * gdim, (g + 1) * gdim)
        u = _gelu_tanh(jnp.dot(x, w_ref[:, cols], preferred_element_type=F32) + b_ref[:, cols])
        s = jnp.dot(ws_ref[g], vn[:, cols], preferred_element_type=F32) + bs_ref[:, g:g + 1]
        o_ref[:, cols] = (u * s).astype(o_ref.dtype)


def _spatial_gating(x_bf16, w_in, b_in, ln_g, ln_b, w_s, b_s):
    n, d = x_bf16.shape
    half = w_in.shape[1] // 2
    gdim = half // SG_GROUPS
    tm = SG_TILE
    reps = tm // SG_CHUNK
    causal = np.tril(np.ones((SG_CHUNK, SG_CHUNK), dtype=bool))
    w_tri = jnp.where(causal[None], w_s, 0.0).astype(BF16)
    eye = jnp.eye(reps, dtype=BF16)
    w_blk = jnp.einsum("ab,gts->gatbs", eye, w_tri).reshape(SG_GROUPS, tm, tm)
    bs_tile = jnp.tile(jnp.transpose(b_s), (reps, 1))
    const2 = lambda i: (0, 0)
    return pl.pallas_call(
        functools.partial(_sg_body, half=half, gdim=gdim),
        out_shape=jax.ShapeDtypeStruct((n, half), BF16),
        grid=(n // tm,),
        in_specs=[pl.BlockSpec((tm, d), lambda i: (i, 0)),
                  pl.BlockSpec((d, 2 * half), const2),
                  pl.BlockSpec((1, 2 * half), const2),
                  pl.BlockSpec((1, half), const2),
                  pl.BlockSpec((1, half), const2),
                  pl.BlockSpec((SG_GROUPS, tm, tm), lambda i: (0, 0, 0)),
                  pl.BlockSpec((tm, SG_GROUPS), const2)],
        out_specs=pl.BlockSpec((tm, half), lambda i: (i, 0)),
        compiler_params=_params(("parallel",)),
        name="spatial_gating",
    )(x_bf16, w_in.astype(BF16), b_in.reshape(1, -1), ln_g.reshape(1, -1),
      ln_b.reshape(1, -1), w_blk, bs_tile)


def kernel(x, rel_bias, attn_w_in, attn_w_out, attn_lambda, attn_subln_g, sg_w_in, sg_b_in,
           sg_ln_g, sg_ln_b, sg_w_s, sg_b_s, sg_w_out, moe_w_group, moe_b_group, moe_w_router,
           moe_b_router, moe_w_gate, moe_w_up, moe_w_down, ln_g, ln_b):
    b, s, d = x.shape
    n = b * s
    h = x.reshape(n, d)

    lam_init = 0.8 - 0.6 * math.exp(-0.3 * 0)
    t = min(ATTN_TILE, s)
    qkv = _qkv_proj(h, attn_w_in[0].astype(BF16)).reshape(b, s, 3 * d)
    bias_tiles = _bias_tiles(rel_bias, t, s)
    o = _diff_attention(qkv, bias_tiles, attn_lambda[0], attn_subln_g[0], lam_init, t)
    wr, br = _router_params(moe_w_group[0], moe_b_group[0], moe_w_router[0], moe_b_router[0])
    h, route, counts = _proj_ln_route(o.reshape(n, d), attn_w_out[0].astype(BF16), h,
                                      ln_g[0, 0], ln_b[0, 0], wr, br)
    h, h_bf16 = _hier_moe_ln(h, route, counts, moe_w_gate, moe_w_up, moe_w_down, 0,
                             ln_g[0, 1], ln_b[0, 1], with_bf16=True)

    gated = _spatial_gating(h_bf16, sg_w_in[0], sg_b_in[0], sg_ln_g[0], sg_ln_b[0],
                            sg_w_s[0], sg_b_s[0])
    wr, br = _router_params(moe_w_group[1], moe_b_group[1], moe_w_router[1], moe_b_router[1])
    h, route, counts = _proj_ln_route(gated, sg_w_out[0].astype(BF16), h,
                                      ln_g[1, 0], ln_b[1, 0], wr, br)
    (h,) = _hier_moe_ln(h, route, counts, moe_w_gate, moe_w_up, moe_w_down, 1,
                        ln_g[1, 1], ln_b[1, 1], with_bf16=False)
    return h.reshape(b, s, d)
```
